```python
import math
import jax, jax.numpy as jnp
from jax import lax
import numpy as np

D_MODEL = 1024
BATCH = 8
SEQ = 8192
DEPTH = 1
DEC_BATCH = 4
DEC_SEQ = 4096
PAST_LEN = 128

D_S5 = D_MODEL // 2
S5_GROUP = 16
S5_GROUPS = D_S5 // S5_GROUP
S5_STATE = 64
N_DIR = 2
D_CONV = D_MODEL // 2
CONV_WIDTH = 3
N_MEM = 256
XATTN_HEADS = 4
XATTN_HEAD_DIM = D_MODEL // XATTN_HEADS
D_FF = -(-8 * D_MODEL // (3 * 256)) * 256
N_GATES = 2
D_IN = D_S5 + 3 * D_CONV + N_GATES * D_MODEL
EPS = 1e-6
DT_MIN = 1e-3
DT_MAX = 1e-1

kernel_name = "hybrid_s5_shortconv_memxattn_encoder"


def rmsnorm(x, g):
    xf = x.astype(jnp.float32)
    r = lax.rsqrt(jnp.mean(xf * xf, axis=-1, keepdims=True) + EPS)
    return (xf * r * g.astype(jnp.float32)).astype(x.dtype)


def s5_combine(e1, e2):
    a1, b1 = e1
    a2, b2 = e2
    return a1 * a2, a2 * b1 + b2


def s5_direction(u, a_re, a_im, log_dt, b_re, b_im, c_re, c_im, reverse):
    f32 = jnp.float32
    A = lax.complex(a_re.astype(f32), a_im.astype(f32))
    dt = jnp.exp(log_dt.astype(f32))[:, None]
    A_bar = jnp.exp(A * dt)
    Bm = lax.complex(b_re.astype(f32), b_im.astype(f32))
    B_bar = ((A_bar - 1.0) / A)[..., None] * Bm
    Cm = lax.complex(c_re.astype(f32), c_im.astype(f32))
    bu = jnp.einsum("gnp,blgp->blgn", B_bar, u.astype(jnp.complex64))
    a = jnp.broadcast_to(A_bar, bu.shape)
    _, h = lax.associative_scan(s5_combine, (a, bu), reverse=reverse, axis=1)
    return jnp.einsum("gpn,blgn->blgp", Cm, h).real


def hybrid_mixer(xn, w_in, b_gate, s5_a_re, s5_a_im, s5_log_dt, s5_b_re, s5_b_im,
                 s5_c_re, s5_c_im, s5_d, w_glu_a, w_glu_b, conv_w, w_conv_out, w_o):
    f32 = jnp.float32
    bsz, L, _ = xn.shape
    proj = jnp.einsum("bld,de->ble", xn, w_in)
    u_s5, x_c, gate_b, gate_c, g = jnp.split(
        proj, [D_S5, D_S5 + D_CONV, D_S5 + 2 * D_CONV, D_S5 + 3 * D_CONV], axis=-1)
    gates = jax.nn.sigmoid((g + b_gate).astype(f32))

    u = u_s5.astype(f32).reshape(bsz, L, S5_GROUPS, S5_GROUP)
    y = (s5_direction(u, s5_a_re[0], s5_a_im[0], s5_log_dt[0], s5_b_re[0], s5_b_im[0],
                      s5_c_re[0], s5_c_im[0], reverse=False)
         + s5_direction(u, s5_a_re[1], s5_a_im[1], s5_log_dt[1], s5_b_re[1], s5_b_im[1],
                        s5_c_re[1], s5_c_im[1], reverse=True))
    y = y.reshape(bsz, L, D_S5) + s5_d.astype(f32) * u_s5.astype(f32)
    y = jax.nn.gelu(y).astype(xn.dtype)
    s5_out = (y @ w_glu_a) * jax.nn.sigmoid(y @ w_glu_b)

    z = gate_c * x_c
    zp = jnp.pad(z, ((0, 0), (1, 1), (0, 0)))
    zc = conv_w[0] * zp[:, :-2] + conv_w[1] * zp[:, 1:-1] + conv_w[2] * zp[:, 2:]
    conv_out = (gate_b * zc) @ w_conv_out

    merged = gates[..., :D_MODEL] * s5_out.astype(f32) + gates[..., D_MODEL:] * conv_out.astype(f32)
    return merged.astype(xn.dtype) @ w_o


def memory_cross_attention(hn, mem, norm_mem_g, w_q, w_kv, w_xo):
    bsz, L, _ = hn.shape
    mn = rmsnorm(mem, norm_mem_g)
    q = (hn @ w_q).reshape(bsz, L, XATTN_HEADS, XATTN_HEAD_DIM)
    k, v = jnp.split(mn @ w_kv, 2, axis=-1)
    k = k.reshape(bsz, N_MEM, XATTN_HEADS, XATTN_HEAD_DIM)
    v = v.reshape(bsz, N_MEM, XATTN_HEADS, XATTN_HEAD_DIM)
    s = jnp.einsum("blhd,bmhd->bhlm", q, k).astype(jnp.float32) * (XATTN_HEAD_DIM ** -0.5)
    p = jax.nn.softmax(s, axis=-1).astype(v.dtype)
    o = jnp.einsum("bhlm,bmhd->blhd", p, v).reshape(bsz, L, D_MODEL)
    return o @ w_xo


def swiglu(hn, w_gate_up, w_down):
    gate, up = jnp.split(hn @ w_gate_up, 2, axis=-1)
    return (jax.nn.silu(gate) * up) @ w_down


def trunk(x, mem, norm_mix_g, w_in, b_gate, s5_a_re, s5_a_im, s5_log_dt, s5_b_re, s5_b_im,
          s5_c_re, s5_c_im, s5_d, w_glu_a, w_glu_b, conv_w, w_conv_out, w_o,
          norm_xattn_g, norm_mem_g, w_q, w_kv, w_xo, norm_ffn_g, w_gate_up, w_down,
          norm_final_g):
    h = x
    for l in range(DEPTH):
        xn = rmsnorm(h, norm_mix_g[l])
        h = h + hybrid_mixer(xn, w_in[l], b_gate[l], s5_a_re[l], s5_a_im[l], s5_log_dt[l],
                             s5_b_re[l], s5_b_im[l], s5_c_re[l], s5_c_im[l], s5_d[l],
                             w_glu_a[l], w_glu_b[l], conv_w[l], w_conv_out[l], w_o[l])
        hn = rmsnorm(h, norm_xattn_g[l])
        h = h + memory_cross_attention(hn, mem, norm_mem_g[l], w_q[l], w_kv[l], w_xo[l])
        hn = rmsnorm(h, norm_ffn_g[l])
        h = h + swiglu(hn, w_gate_up[l], w_down[l])
    return rmsnorm(h, norm_final_g)


def setup_inputs(seed: int = 0) -> dict:
    key = jax.random.key(seed)
    ks = jax.random.split(key, 32)
    f32 = jnp.float32
    nrm = lambda k, shape, scale: jax.random.normal(k, shape, f32) * scale
    gain = lambda k, shape: 1.0 + 0.02 * jax.random.normal(k, shape, f32)
    sd = (DEPTH, N_DIR, S5_GROUPS, S5_STATE)
    n_idx = jnp.arange(S5_STATE, dtype=f32)
    return {
        "x_prompt": nrm(ks[0], (BATCH, SEQ, D_MODEL), 1.0),
        "x_sample": nrm(ks[1], (DEC_BATCH, DEC_SEQ, D_MODEL), 1.0),
        "mem_prompt": nrm(ks[2], (BATCH, N_MEM, D_MODEL), 1.0),
        "mem_sample": nrm(ks[3], (DEC_BATCH, N_MEM, D_MODEL), 1.0),
        "norm_mix_g": gain(ks[4], (DEPTH, D_MODEL)),
        "w_in": nrm(ks[5], (DEPTH, D_MODEL, D_IN), D_MODEL ** -0.5),
        "b_gate": nrm(ks[6], (DEPTH, N_GATES * D_MODEL), 0.01),
        "s5_a_re": -0.5 + 0.01 * jax.random.normal(ks[7], sd, f32),
        "s5_a_im": math.pi * n_idx + 0.01 * jax.random.normal(ks[8], sd, f32),
        "s5_log_dt": jax.random.uniform(ks[9], (DEPTH, N_DIR, S5_GROUPS), f32,
                                        math.log(DT_MIN), math.log(DT_MAX)),
        "s5_b_re": nrm(ks[10], sd + (S5_GROUP,), (2 * S5_GROUP) ** -0.5),
        "s5_b_im": nrm(ks[11], sd + (S5_GROUP,), (2 * S5_GROUP) ** -0.5),
        "s5_c_re": nrm(ks[12], (DEPTH, N_DIR, S5_GROUPS, S5_GROUP, S5_STATE), (2 * S5_STATE) ** -0.5),
        "s5_c_im": nrm(ks[13], (DEPTH, N_DIR, S5_GROUPS, S5_GROUP, S5_STATE), (2 * S5_STATE) ** -0.5),
        "s5_d": nrm(ks[14], (DEPTH, D_S5), 1.0),
        "w_glu_a": nrm(ks[15], (DEPTH, D_S5, D_MODEL), D_S5 ** -0.5),
        "w_glu_b": nrm(ks[16], (DEPTH, D_S5, D_MODEL), D_S5 ** -0.5),
        "conv_w": nrm(ks[17], (DEPTH, CONV_WIDTH, D_CONV), CONV_WIDTH ** -0.5),
        "w_conv_out": nrm(ks[18], (DEPTH, D_CONV, D_MODEL), D_CONV ** -0.5),
        "w_o": nrm(ks[19], (DEPTH, D_MODEL, D_MODEL), D_MODEL ** -0.5),
        "norm_xattn_g": gain(ks[20], (DEPTH, D_MODEL)),
        "norm_mem_g": gain(ks[21], (DEPTH, D_MODEL)),
        "w_q": nrm(ks[22], (DEPTH, D_MODEL, D_MODEL), D_MODEL ** -0.5),
        "w_kv": nrm(ks[23], (DEPTH, D_MODEL, 2 * D_MODEL), D_MODEL ** -0.5),
        "w_xo": nrm(ks[24], (DEPTH, D_MODEL, D_MODEL), D_MODEL ** -0.5),
        "norm_ffn_g": gain(ks[25], (DEPTH, D_MODEL)),
        "w_gate_up": nrm(ks[26], (DEPTH, D_MODEL, 2 * D_FF), D_MODEL ** -0.5),
        "w_down": nrm(ks[27], (DEPTH, D_FF, D_MODEL), D_FF ** -0.5),
        "norm_final_g": gain(ks[28], (D_MODEL,)),
    }


def reference(x_prompt, x_sample, mem_prompt, mem_sample, norm_mix_g, w_in, b_gate,
              s5_a_re, s5_a_im, s5_log_dt, s5_b_re, s5_b_im, s5_c_re, s5_c_im, s5_d,
              w_glu_a, w_glu_b, conv_w, w_conv_out, w_o, norm_xattn_g, norm_mem_g,
              w_q, w_kv, w_xo, norm_ffn_g, w_gate_up, w_down, norm_final_g):
    weights = (norm_mix_g, w_in, b_gate, s5_a_re, s5_a_im, s5_log_dt, s5_b_re, s5_b_im,
               s5_c_re, s5_c_im, s5_d, w_glu_a, w_glu_b, conv_w, w_conv_out, w_o,
               norm_xattn_g, norm_mem_g, w_q, w_kv, w_xo, norm_ffn_g, w_gate_up, w_down,
               norm_final_g)
    y_prompt = trunk(x_prompt, mem_prompt, *weights)
    y_sample = trunk(x_sample, mem_sample, *weights)
    return (y_prompt, y_sample)
```

```python
import functools
import math

import jax
import jax.numpy as jnp
from jax import lax
from jax.experimental import pallas as pl
from jax.experimental.pallas import tpu as pltpu

F32 = jnp.float32
BF16 = jnp.bfloat16

D_MODEL = 1024
D_S5 = D_MODEL // 2
S5_GROUP = 16
S5_GROUPS = D_S5 // S5_GROUP
S5_STATE = 64
D_CONV = D_MODEL // 2
N_MEM = 256
XATTN_HEADS = 4
XATTN_HEAD_DIM = D_MODEL // XATTN_HEADS
D_FF = -(-8 * D_MODEL // (3 * 256)) * 256
EPS = 1e-6

S5_LANES = S5_GROUPS * S5_STATE
SUBLANES = 8
VMEM_LIMIT = 56 * 1024 * 1024

TM = 512
S5_STEPS = 64
SCAN_LANES = 512
FF_CHUNK = D_FF // 2


def _const_spec(shape):
    zeros = (0,) * len(shape)
    return pl.BlockSpec(shape, lambda *_: zeros, pipeline_mode=pl.Buffered(1))


def _rms(x, g):
    ms = jnp.mean(x * x, axis=-1, keepdims=True)
    return x * lax.rsqrt(ms + EPS) * g


def _sigmoid(x):
    return 1.0 / (1.0 + jnp.exp(-x))


def _gelu_tanh(x):
    c = math.sqrt(2.0 / math.pi)
    return 0.5 * x * (1.0 + jnp.tanh(c * (x + 0.044715 * (x * x * x))))


def _dot(a, b):
    return jnp.dot(a, b, preferred_element_type=F32)


def _in_proj_kernel(x_ref, g_ref, w_ref, bg_ref, u_ref, z_ref, gb_ref, gates_ref):
    xn = _rms(x_ref[...], g_ref[...]).astype(BF16)
    u_ref[...] = _dot(xn, w_ref[:, 0:D_S5]).astype(BF16)
    x_c = _dot(xn, w_ref[:, D_S5:D_S5 + D_CONV])
    gb_ref[...] = _dot(xn, w_ref[:, D_S5 + D_CONV:D_S5 + 2 * D_CONV]).astype(BF16)
    gate_c = _dot(xn, w_ref[:, D_S5 + 2 * D_CONV:D_S5 + 3 * D_CONV])
    z_ref[...] = (gate_c * x_c).astype(BF16)
    g = _dot(xn, w_ref[:, D_S5 + 3 * D_CONV:])
    gates_ref[...] = _sigmoid(g + bg_ref[...]).astype(BF16)


def _in_proj(x2d, norm_g, w_in, b_gate):
    n = x2d.shape[0]
    d_in = w_in.shape[1]
    row = lambda w: pl.BlockSpec((TM, w), lambda i: (i, 0))
    return pl.pallas_call(
        _in_proj_kernel,
        grid=(n // TM,),
        in_specs=[row(D_MODEL), _const_spec((1, D_MODEL)), _const_spec((D_MODEL, d_in)),
                  _const_spec((1, 2 * D_MODEL))],
        out_specs=[row(D_S5), row(D_CONV), row(D_CONV), row(2 * D_MODEL)],
        out_shape=[jax.ShapeDtypeStruct((n, D_S5), BF16), jax.ShapeDtypeStruct((n, D_CONV), BF16),
                   jax.ShapeDtypeStruct((n, D_CONV), BF16), jax.ShapeDtypeStruct((n, 2 * D_MODEL), BF16)],
        compiler_params=pltpu.CompilerParams(dimension_semantics=("parallel",), vmem_limit_bytes=VMEM_LIMIT),
        name="in_proj",
    )(x2d, norm_g, w_in, b_gate)


def _mem_kv_kernel(m_ref, g_ref, w_ref, k_ref, v_ref):
    mn = _rms(m_ref[0], g_ref[...]).astype(BF16)
    k_ref[0] = _dot(mn, w_ref[:, :D_MODEL]).astype(BF16)
    v_ref[0] = _dot(mn, w_ref[:, D_MODEL:]).astype(BF16)


def _mem_kv(mem, norm_g, w_kv):
    b = mem.shape[0]
    blk = pl.BlockSpec((1, N_MEM, D_MODEL), lambda i: (i, 0, 0))
    return pl.pallas_call(
        _mem_kv_kernel,
        grid=(b,),
        in_specs=[blk, _const_spec((1, D_MODEL)), _const_spec((D_MODEL, 2 * D_MODEL))],
        out_specs=[blk, blk],
        out_shape=[jax.ShapeDtypeStruct((b, N_MEM, D_MODEL), BF16)] * 2,
        compiler_params=pltpu.CompilerParams(dimension_semantics=("parallel",), vmem_limit_bytes=VMEM_LIMIT),
        name="mem_kv",
    )(mem, norm_g, w_kv)


B_CHUNKS = D_S5 // 128
B_CHUNK_LANES = S5_LANES // B_CHUNKS
C_CHUNKS = D_S5 // 256
C_CHUNK_LANES = S5_LANES // C_CHUNKS


def _s5_weights(a_re, a_im, log_dt, b_re, b_im, c_re, c_im):
    a = lax.complex(a_re.astype(F32), a_im.astype(F32))
    dt = jnp.exp(log_dt.astype(F32))[:, None]
    a_bar = jnp.exp(a * dt)
    b_bar = ((a_bar - 1.0) / a)[..., None] * lax.complex(b_re.astype(F32), b_im.astype(F32))
    eye = jnp.eye(S5_GROUPS, dtype=F32)

    def in_blockdiag(w):
        full = jnp.einsum("gnp,gh->gphn", w, eye).reshape(D_S5, S5_LANES)
        return jnp.stack([full[c * 128:(c + 1) * 128, c * B_CHUNK_LANES:(c + 1) * B_CHUNK_LANES]
                          for c in range(B_CHUNKS)])

    def out_blockdiag(w):
        full = jnp.einsum("gpn,gh->gnhp", w, eye).reshape(S5_LANES, D_S5)
        return jnp.stack([full[c * C_CHUNK_LANES:(c + 1) * C_CHUNK_LANES, c * 256:(c + 1) * 256]
                          for c in range(C_CHUNKS)])

    wb = jnp.concatenate([in_blockdiag(b_bar.real), in_blockdiag(b_bar.imag)], axis=-1).astype(BF16)
    wc_re = out_blockdiag(c_re.astype(F32)).astype(BF16)
    wc_im = out_blockdiag(-c_im.astype(F32)).astype(BF16)
    a_rows = jnp.stack([a_bar.real.reshape(S5_LANES), a_bar.imag.reshape(S5_LANES)])
    return wb, wc_re, wc_im, a_rows


def _s5_kernel(uf_ref, ub_ref, wbf_ref, wbb_ref, wcref_ref, wcimf_ref, wcreb_ref, wcimb_ref,
               af_ref, ab_ref, yf_ref, yb_ref, bu_re, bu_im, hs_re, hs_im, carry_ref):
    steps = uf_ref.shape[0] // SUBLANES

    @pl.when(pl.program_id(0) == 0)
    def _():
        carry_ref[...] = jnp.zeros_like(carry_ref)

    for d, (u_ref, wb_ref) in enumerate(((uf_ref, wbf_ref), (ub_ref, wbb_ref))):
        for c in range(B_CHUNKS):
            r = _dot(u_ref[:, c * 128:(c + 1) * 128], wb_ref[c])
            bu_re[d, :, c * B_CHUNK_LANES:(c + 1) * B_CHUNK_LANES] = r[:, :B_CHUNK_LANES]
            bu_im[d, :, c * B_CHUNK_LANES:(c + 1) * B_CHUNK_LANES] = r[:, B_CHUNK_LANES:]

    for lc in range(S5_LANES // SCAN_LANES):
        sl = slice(lc * SCAN_LANES, (lc + 1) * SCAN_LANES)
        bcast = lambda ref, r: jnp.broadcast_to(ref[r:r + 1, sl], (SUBLANES, SCAN_LANES))
        afr, afi, abr, abi = bcast(af_ref, 0), bcast(af_ref, 1), bcast(ab_ref, 0), bcast(ab_ref, 1)

        def step(h_re, h_im, a_re, a_im, d, row):
            b_re = bu_re[d, pl.ds(row, SUBLANES), sl]
            b_im = bu_im[d, pl.ds(row, SUBLANES), sl]
            return a_re * h_re - a_im * h_im + b_re, a_re * h_im + a_im * h_re + b_im

        def body(j, carry):
            hfr, hfi, hbr, hbi = carry
            rf = pl.multiple_of(j * (2 * SUBLANES), 2 * SUBLANES)
            f0r, f0i = step(hfr, hfi, afr, afi, 0, rf)
            f1r, f1i = step(f0r, f0i, afr, afi, 0, rf + SUBLANES)
            hs_re[0, pl.ds(rf, 2 * SUBLANES), sl] = jnp.concatenate([f0r, f1r], axis=0).astype(BF16)
            hs_im[0, pl.ds(rf, 2 * SUBLANES), sl] = jnp.concatenate([f0i, f1i], axis=0).astype(BF16)
            rb = pl.multiple_of((steps - 2) * SUBLANES - j * (2 * SUBLANES), 2 * SUBLANES)
            b1r, b1i = step(hbr, hbi, abr, abi, 1, rb + SUBLANES)
            b0r, b0i = step(b1r, b1i, abr, abi, 1, rb)
            hs_re[1, pl.ds(rb, 2 * SUBLANES), sl] = jnp.concatenate([b0r, b1r], axis=0).astype(BF16)
            hs_im[1, pl.ds(rb, 2 * SUBLANES), sl] = jnp.concatenate([b0i, b1i], axis=0).astype(BF16)
            return f1r, f1i, b0r, b0i

        init = tuple(carry_ref[k, :, sl] for k in range(4))
        final = lax.fori_loop(0, steps // 2, body, init)
        for k in range(4):
            carry_ref[k, :, sl] = final[k]

    for d, (y_ref, wcre_ref, wcim_ref) in enumerate(((yf_ref, wcref_ref, wcimf_ref),
                                                      (yb_ref, wcreb_ref, wcimb_ref))):
        for c in range(C_CHUNKS):
            ks = slice(c * C_CHUNK_LANES, (c + 1) * C_CHUNK_LANES)
            y = _dot(hs_re[d, :, ks], wcre_ref[c]) + _dot(hs_im[d, :, ks], wcim_ref[c])
            y_ref[:, c * 256:(c + 1) * 256] = y.astype(BF16)


def _s5_scan(u_tb, wf, wb):
    rows = S5_STEPS * SUBLANES
    n = u_tb.shape[0] // rows
    fwd = pl.BlockSpec((rows, D_S5), lambda i: (i, 0))
    bwd = pl.BlockSpec((rows, D_S5), lambda i: (n - 1 - i, 0))
    wspecs = lambda: [_const_spec((B_CHUNKS, 128, 2 * B_CHUNK_LANES))] * 2 + \
        [_const_spec((C_CHUNKS, C_CHUNK_LANES, 256))] * 4 + [_const_spec((2, S5_LANES))] * 2
    return pl.pallas_call(
        _s5_kernel,
        grid=(n,),
        in_specs=[fwd, bwd] + wspecs(),
        out_specs=[fwd, bwd],
        out_shape=[jax.ShapeDtypeStruct(u_tb.shape, BF16)] * 2,
        scratch_shapes=[pltpu.VMEM((2, rows, S5_LANES), F32), pltpu.VMEM((2, rows, S5_LANES), F32),
                        pltpu.VMEM((2, rows, S5_LANES), BF16), pltpu.VMEM((2, rows, S5_LANES), BF16),
                        pltpu.VMEM((4, SUBLANES, S5_LANES), F32)],
        compiler_params=pltpu.CompilerParams(dimension_semantics=("arbitrary",), vmem_limit_bytes=VMEM_LIMIT),
        name="s5_scan",
    )(u_tb, u_tb, wf[0], wb[0], wf[1], wf[2], wb[1], wb[2], wf[3], wb[3])


def _mix_attn_kernel(x_ref, yf_ref, yb_ref, u_ref, z_ref, zprev_ref, znext_ref, gb_ref, gates_ref,
                     kt_ref, v_ref, d_ref, wga_ref, wgb_ref, cw_ref, wco_ref, wo_ref, gx_ref, wq_ref,
                     wxo_ref, h_ref):
    i = pl.program_id(1)
    n_i = pl.num_programs(1)

    u = u_ref[0].astype(F32)
    y = yf_ref[0].astype(F32) + yb_ref[0].astype(F32) + d_ref[...] * u
    y = _gelu_tanh(y).astype(BF16)
    s5_out = _dot(y, wga_ref[...]) * _sigmoid(_dot(y, wgb_ref[...]))

    z = z_ref[0].astype(F32)
    rows = lax.broadcasted_iota(jnp.int32, z.shape, 0)
    z_prev = jnp.where(i > 0, zprev_ref[0, 15:16, :].astype(F32), 0.0)
    z_next = jnp.where(i < n_i - 1, znext_ref[0, 0:1, :].astype(F32), 0.0)
    z_m1 = jnp.where(rows == 0, z_prev, pltpu.roll(z, 1, axis=0))
    z_p1 = jnp.where(rows == TM - 1, z_next, pltpu.roll(z, TM - 1, axis=0))
    zc = cw_ref[0:1, :] * z_m1 + cw_ref[1:2, :] * z + cw_ref[2:3, :] * z_p1
    conv_out = _dot((gb_ref[0].astype(F32) * zc).astype(BF16), wco_ref[...])

    merged = (gates_ref[0, :, :D_MODEL].astype(F32) * s5_out
              + gates_ref[0, :, D_MODEL:].astype(F32) * conv_out)
    h = x_ref[0] + _dot(merged.astype(BF16), wo_ref[...])

    hn = _rms(h, gx_ref[...]).astype(BF16)
    q = _dot(hn, wq_ref[...]).astype(BF16)
    heads = []
    for hd in range(XATTN_HEADS):
        hs = slice(hd * XATTN_HEAD_DIM, (hd + 1) * XATTN_HEAD_DIM)
        s = _dot(q[:, hs], kt_ref[0, hs, :]) * (XATTN_HEAD_DIM ** -0.5)
        e = jnp.exp(s - jnp.max(s, axis=-1, keepdims=True))
        p = (e / jnp.sum(e, axis=-1, keepdims=True)).astype(BF16)
        heads.append(_dot(p, v_ref[0, :, hs]).astype(BF16))
    o = jnp.concatenate(heads, axis=-1)
    h_ref[0] = h + _dot(o, wxo_ref[...])


def _mix_attn(x, yf, yb, u, z, gb, gates, kt, v, s5_d, w_glu_a, w_glu_b, conv_w, w_conv_out, w_o,
              norm_xattn_g, w_q, w_xo):
    b, l, _ = x.shape
    n_i = l // TM
    tile = lambda w: pl.BlockSpec((1, TM, w), lambda bi, i: (bi, i, 0))
    halo = TM // 16
    z_prev = pl.BlockSpec((1, 16, D_CONV), lambda bi, i: (bi, jnp.maximum(i * halo - 1, 0), 0))
    z_next = pl.BlockSpec((1, 16, D_CONV), lambda bi, i: (bi, jnp.minimum((i + 1) * halo, l // 16 - 1), 0))
    per_batch = lambda r, c: pl.BlockSpec((1, r, c), lambda bi, i: (bi, 0, 0))
    return pl.pallas_call(
        _mix_attn_kernel,
        grid=(b, n_i),
        in_specs=[tile(D_MODEL), tile(D_S5), tile(D_S5), tile(D_S5), tile(D_CONV), z_prev, z_next,
                  tile(D_CONV), tile(2 * D_MODEL), per_batch(D_MODEL, N_MEM), per_batch(N_MEM, D_MODEL),
                  _const_spec((1, D_S5)), _const_spec((D_S5, D_MODEL)), _const_spec((D_S5, D_MODEL)),
                  _const_spec((3, D_CONV)), _const_spec((D_CONV, D_MODEL)), _const_spec((D_MODEL, D_MODEL)),
                  _const_spec((1, D_MODEL)), _const_spec((D_MODEL, D_MODEL)), _const_spec((D_MODEL, D_MODEL))],
        out_specs=tile(D_MODEL),
        out_shape=jax.ShapeDtypeStruct(x.shape, F32),
        compiler_params=pltpu.CompilerParams(dimension_semantics=("parallel", "parallel"),
                                             vmem_limit_bytes=VMEM_LIMIT),
        name="mix_attn",
    )(x, yf, yb, u, z, z, z, gb, gates, kt, v, s5_d, w_glu_a, w_glu_b, conv_w, w_conv_out, w_o,
      norm_xattn_g, w_q, w_xo)


def _ffn_kernel(h_ref, gf_ref, wgu_ref, wd_ref, gout_ref, o_ref):
    h = h_ref[...]
    hn = _rms(h, gf_ref[...]).astype(BF16)
    acc = h
    for c in range(D_FF // FF_CHUNK):
        gate = _dot(hn, wgu_ref[:, c * FF_CHUNK:(c + 1) * FF_CHUNK])
        up = _dot(hn, wgu_ref[:, D_FF + c * FF_CHUNK:D_FF + (c + 1) * FF_CHUNK])
        act = (gate * _sigmoid(gate) * up).astype(BF16)
        acc = acc + _dot(act, wd_ref[c * FF_CHUNK:(c + 1) * FF_CHUNK, :])
    o_ref[...] = _rms(acc, gout_ref[...])


def _ffn_out(h2d, norm_ffn_g, w_gate_up, w_down, norm_final_g):
    n = h2d.shape[0]
    row = pl.BlockSpec((TM, D_MODEL), lambda i: (i, 0))
    return pl.pallas_call(
        _ffn_kernel,
        grid=(n // TM,),
        in_specs=[row, _const_spec((1, D_MODEL)), _const_spec((D_MODEL, 2 * D_FF)),
                  _const_spec((D_FF, D_MODEL)), _const_spec((1, D_MODEL))],
        out_specs=row,
        out_shape=jax.ShapeDtypeStruct(h2d.shape, F32),
        compiler_params=pltpu.CompilerParams(dimension_semantics=("parallel",), vmem_limit_bytes=VMEM_LIMIT),
        name="ffn_out",
    )(h2d, norm_ffn_g, w_gate_up, w_down, norm_final_g)


def _to_time_major(a):
    b, l, c = a.shape
    a = jnp.pad(a, ((0, SUBLANES - b), (0, 0), (0, 0)))
    return jnp.swapaxes(a, 0, 1).reshape(l * SUBLANES, c)


def _from_time_major(a, b):
    l = a.shape[0] // SUBLANES
    return jnp.swapaxes(a.reshape(l, SUBLANES, a.shape[1]), 0, 1)[:b]


def _trunk(x, mem, w):
    b, l, _ = x.shape
    u, z, gb, gates = _in_proj(x.reshape(b * l, D_MODEL), w["norm_mix_g"], w["w_in"], w["b_gate"])
    k, v = _mem_kv(mem, w["norm_mem_g"], w["w_kv"])
    kt = jnp.swapaxes(k, 1, 2)
    yf, yb = _s5_scan(_to_time_major(u.reshape(b, l, D_S5)), w["s5_fwd"], w["s5_bwd"])
    yf, yb = _from_time_major(yf, b), _from_time_major(yb, b)
    shp = lambda a: a.reshape(b, l, a.shape[-1])
    h = _mix_attn(x, yf, yb, shp(u), shp(z), shp(gb), shp(gates), kt, v, w["s5_d"], w["w_glu_a"],
                  w["w_glu_b"], w["conv_w"], w["w_conv_out"], w["w_o"], w["norm_xattn_g"], w["w_q"], w["w_xo"])
    out = _ffn_out(h.reshape(b * l, D_MODEL), w["norm_ffn_g"], w["w_gate_up"], w["w_down"], w["norm_final_g"])
    return out.reshape(b, l, D_MODEL)


def kernel(x_prompt, x_sample, mem_prompt, mem_sample, norm_mix_g, w_in, b_gate, s5_a_re, s5_a_im, s5_log_dt, s5_b_re, s5_b_im, s5_c_re, s5_c_im, s5_d, w_glu_a, w_glu_b, conv_w, w_conv_out, w_o, norm_xattn_g, norm_mem_g, w_q, w_kv, w_xo, norm_ffn_g, w_gate_up, w_down, norm_final_g):
    row = lambda a: a.reshape(1, -1).astype(F32)
    s5 = lambda d: _s5_weights(s5_a_re[0, d], s5_a_im[0, d], s5_log_dt[0, d], s5_b_re[0, d], s5_b_im[0, d],
                               s5_c_re[0, d], s5_c_im[0, d])
    w = {
        "norm_mix_g": row(norm_mix_g[0]), "w_in": w_in[0].astype(BF16), "b_gate": row(b_gate[0]),
        "s5_fwd": s5(0), "s5_bwd": s5(1), "s5_d": row(s5_d[0]),
        "w_glu_a": w_glu_a[0].astype(BF16), "w_glu_b": w_glu_b[0].astype(BF16),
        "conv_w": conv_w[0].astype(F32), "w_conv_out": w_conv_out[0].astype(BF16), "w_o": w_o[0].astype(BF16),
        "norm_xattn_g": row(norm_xattn_g[0]), "norm_mem_g": row(norm_mem_g[0]),
        "w_q": w_q[0].astype(BF16), "w_kv": w_kv[0].astype(BF16), "w_xo": w_xo[0].astype(BF16),
        "norm_ffn_g": row(norm_ffn_g[0]), "w_gate_up": w_gate_up[0].astype(BF16),
        "w_down": w_down[0].astype(BF16), "norm_final_g": row(norm_final_g),
    }
    return _trunk(x_prompt, mem_prompt, w), _trunk(x_sample, mem_sample, w)
```

```python
import math

import jax
import jax.numpy as jnp
from jax import lax
from jax.experimental import pallas as pl
from jax.experimental.pallas import tpu as pltpu

F32 = jnp.float32
BF16 = jnp.bfloat16

D_MODEL = 1024
D_S5 = D_MODEL // 2
S5_GROUP = 16
S5_GROUPS = D_S5 // S5_GROUP
S5_STATE = 64
D_CONV = D_MODEL // 2
N_MEM = 256
XATTN_HEADS = 4
XATTN_HEAD_DIM = D_MODEL // XATTN_HEADS
D_FF = -(-8 * D_MODEL // (3 * 256)) * 256
EPS = 1e-6

LANES = 128
SUBLANES = 8
VMEM_LIMIT = 56 * 1024 * 1024

TM = 512
CHUNK = LANES
STATE_LANES = 2 * S5_STATE
GK = S5_GROUP * CHUNK
FF_CHUNK = D_FF // 2


def _const_spec(shape):
    zeros = (0,) * len(shape)
    return pl.BlockSpec(shape, lambda *_: zeros, pipeline_mode=pl.Buffered(1))


def _rms(x, g):
    ms = jnp.mean(x * x, axis=-1, keepdims=True)
    return x * lax.rsqrt(ms + EPS) * g


def _sigmoid(x):
    return 1.0 / (1.0 + jnp.exp(-x))


def _gelu_tanh(x):
    c = math.sqrt(2.0 / math.pi)
    return 0.5 * x * (1.0 + jnp.tanh(c * (x + 0.044715 * (x * x * x))))


def _dot(a, b):
    return jnp.dot(a, b, preferred_element_type=F32)


def _in_proj_kernel(x_ref, g_ref, wut_ref, w_ref, bg_ref, ut_ref, z_ref, gb_ref, gates_ref):
    xn = _rms(x_ref[...], g_ref[...]).astype(BF16)
    ut_ref[...] = lax.dot_general(wut_ref[...], xn, (((1,), (1,)), ((), ())), preferred_element_type=F32)
    x_c = _dot(xn, w_ref[:, 0:D_CONV])
    gb_ref[...] = _dot(xn, w_ref[:, D_CONV:2 * D_CONV]).astype(BF16)
    gate_c = _dot(xn, w_ref[:, 2 * D_CONV:3 * D_CONV])
    z_ref[...] = (gate_c * x_c).astype(BF16)
    g = _dot(xn, w_ref[:, 3 * D_CONV:])
    gates_ref[...] = _sigmoid(g + bg_ref[...]).astype(BF16)


def _in_proj(x2d, norm_g, w_ut, w_rest, b_gate):
    n = x2d.shape[0]
    row = lambda w: pl.BlockSpec((TM, w), lambda i: (i, 0))
    return pl.pallas_call(
        _in_proj_kernel,
        grid=(n // TM,),
        in_specs=[row(D_MODEL), _const_spec((1, D_MODEL)), _const_spec(w_ut.shape), _const_spec(w_rest.shape),
                  _const_spec((1, 2 * D_MODEL))],
        out_specs=[pl.BlockSpec((D_S5, TM), lambda i: (0, i)), row(D_CONV), row(D_CONV), row(2 * D_MODEL)],
        out_shape=[jax.ShapeDtypeStruct((D_S5, n), F32), jax.ShapeDtypeStruct((n, D_CONV), BF16),
                   jax.ShapeDtypeStruct((n, D_CONV), BF16), jax.ShapeDtypeStruct((n, 2 * D_MODEL), BF16)],
        compiler_params=pltpu.CompilerParams(dimension_semantics=("parallel",), vmem_limit_bytes=VMEM_LIMIT),
        name="in_proj",
    )(x2d, norm_g, w_ut, w_rest, b_gate)


def _mem_kv_kernel(m_ref, g_ref, w_ref, k_ref, v_ref):
    mn = _rms(m_ref[0], g_ref[...]).astype(BF16)
    k_ref[0] = _dot(mn, w_ref[:, :D_MODEL]).astype(BF16)
    v_ref[0] = _dot(mn, w_ref[:, D_MODEL:]).astype(BF16)


def _mem_kv(mem, norm_g, w_kv):
    b = mem.shape[0]
    blk = pl.BlockSpec((1, N_MEM, D_MODEL), lambda i: (i, 0, 0))
    return pl.pallas_call(
        _mem_kv_kernel,
        grid=(b,),
        in_specs=[blk, _const_spec((1, D_MODEL)), _const_spec((D_MODEL, 2 * D_MODEL))],
        out_specs=[blk, blk],
        out_shape=[jax.ShapeDtypeStruct((b, N_MEM, D_MODEL), BF16)] * 2,
        compiler_params=pltpu.CompilerParams(dimension_semantics=("parallel",), vmem_limit_bytes=VMEM_LIMIT),
        name="mem_kv",
    )(mem, norm_g, w_kv)


def _s5_tables(a_re, a_im, log_dt, b_re, b_im, c_re, c_im):
    f = lambda v: v.astype(F32)
    a_re, a_im, b_re, b_im, c_re, c_im = map(f, (a_re, a_im, b_re, b_im, c_re, c_im))
    dt = jnp.exp(f(log_dt))[..., None]
    lr, li = a_re * dt, a_im * dt

    def apow(k):
        m = jnp.exp(lr[..., None] * k)
        return m * jnp.cos(li[..., None] * k), m * jnp.sin(li[..., None] * k)

    ab_re, ab_im = jnp.exp(lr) * jnp.cos(li), jnp.exp(lr) * jnp.sin(li)
    x, y, den = ab_re - 1.0, ab_im, a_re * a_re + a_im * a_im
    cf_re, cf_im = (x * a_re + y * a_im) / den, (y * a_re - x * a_im) / den
    bb_re = cf_re[..., None] * b_re - cf_im[..., None] * b_im
    bb_im = cf_re[..., None] * b_im + cf_im[..., None] * b_re
    bq_re, bq_im = jnp.swapaxes(bb_re, -1, -2), jnp.swapaxes(bb_im, -1, -2)
    g = a_re.shape[1]

    cbr = c_re[:, :, None] * bq_re[:, :, :, None] - c_im[:, :, None] * bq_im[:, :, :, None]
    cbi = c_re[:, :, None] * bq_im[:, :, :, None] + c_im[:, :, None] * bq_re[:, :, :, None]
    half = jnp.concatenate([cbr, -cbi], axis=-1).reshape(2, g, S5_GROUP * S5_GROUP, STATE_LANES)
    cb = jnp.concatenate([half[0], half[1]], axis=-1)
    j = jnp.arange(2 * CHUNK, dtype=F32)
    pf_re, pf_im = apow(jnp.maximum(j - CHUNK, 0.0))
    pb_re, pb_im = apow(jnp.maximum(CHUNK - j, 0.0))
    mf, mb = (j >= CHUNK).astype(F32), (j <= CHUNK).astype(F32)
    pw = jnp.concatenate([pf_re[0] * mf, pf_im[0] * mf, pb_re[1] * mb, pb_im[1] * mb], axis=1)

    s = jnp.arange(CHUNK, dtype=F32)
    in_re, in_im = apow(jnp.stack([CHUNK - 1.0 - s, s])[:, None, None, :])
    out_re, out_im = apow(jnp.stack([s + 1.0, CHUNK - s])[:, None, None, :])
    in_re, in_im = jnp.swapaxes(in_re, -1, -2), jnp.swapaxes(in_im, -1, -2)
    ws_re = bq_re[:, :, :, None] * in_re[:, :, None] - bq_im[:, :, :, None] * in_im[:, :, None]
    ws_im = bq_re[:, :, :, None] * in_im[:, :, None] + bq_im[:, :, :, None] * in_re[:, :, None]
    ws = jnp.concatenate([ws_re[0], ws_im[0], ws_re[1], ws_im[1]], axis=-1).reshape(g, GK, 2 * STATE_LANES)
    cn_re, cn_im = jnp.swapaxes(c_re, -1, -2), jnp.swapaxes(c_im, -1, -2)
    m_re = cn_re[..., None] * out_re[:, :, :, None] - cn_im[..., None] * out_im[:, :, :, None]
    m_im = cn_re[..., None] * out_im[:, :, :, None] + cn_im[..., None] * out_re[:, :, :, None]
    wc = jnp.concatenate([m_re[0], -m_im[0], m_re[1], -m_im[1]], axis=1).reshape(g, 2 * STATE_LANES, GK)

    ks = CHUNK * (2.0 ** jnp.arange(6, dtype=F32))
    t_re, t_im = apow(ks)
    t_re, t_im = jnp.moveaxis(t_re, -1, 2), jnp.moveaxis(t_im, -1, 2)
    row_a = jnp.concatenate([t_re[0], t_re[0], t_re[1], t_re[1]], axis=-1)
    row_b = jnp.concatenate([-t_im[0], t_im[0], -t_im[1], t_im[1]], axis=-1)
    step = jnp.concatenate([row_a, row_b], axis=1)
    return cb, pw, ws.astype(BF16), wc.astype(BF16), step


def _chunk_pitch(nc):
    p = -(-nc // SUBLANES) * SUBLANES
    return p if (p // SUBLANES) % 2 == 1 else p + SUBLANES


def _make_s5_kernel(seqs):
    nc = sum(n for n, _ in seqs)
    pitch = _chunk_pitch(nc)

    def kernel(*refs):
        n_in = len(seqs)
        ut_refs = refs[:n_in]
        cb_ref, pw_ref, ws_ref, wc_ref, step_ref = refs[n_in:n_in + 5]
        yt_refs = refs[n_in + 5:2 * n_in + 5]
        planes, xcat, toep, lag = refs[2 * n_in + 5:]

        def tiles(refs_):
            for ref, (n, _) in zip(refs_, seqs):
                for c in range(n):
                    yield ref, c

        for r8 in range(S5_GROUP // SUBLANES):
            for row, (ref, c) in enumerate(tiles(ut_refs)):
                planes[pl.ds(r8 * SUBLANES * pitch + row, SUBLANES, stride=pitch), :] = (
                    ref[r8 * SUBLANES:(r8 + 1) * SUBLANES, c * LANES:(c + 1) * LANES])
        for q in range(S5_GROUP):
            xcat[:, q * LANES:(q + 1) * LANES] = planes[q * pitch:q * pitch + nc, :].astype(BF16)

        lag[...] = jnp.dot(cb_ref[0], pw_ref[0], preferred_element_type=F32, precision=lax.Precision.HIGHEST)

        def toeplitz_rows(q, carry):
            for p in range(S5_GROUP):
                row = lag[pl.ds(q * S5_GROUP + p, 1), :]
                skew = pltpu.roll(jnp.broadcast_to(row, (CHUNK, 2 * CHUNK)), 0, axis=1, stride=1, stride_axis=0)
                toep[pl.ds(pl.multiple_of(q * CHUNK, CHUNK), CHUNK), p * LANES:(p + 1) * LANES] = (
                    skew[:, CHUNK:].astype(BF16))
            return carry

        lax.fori_loop(0, S5_GROUP, toeplitz_rows, 0)

        x = xcat[...]
        y = _dot(x, toep[...])
        summ = _dot(x, ws_ref[0])

        idx = lax.broadcasted_iota(jnp.int32, (nc, LANES), 0)
        pos, left, start = idx, idx, 0
        for n, per_seq in seqs:
            inside = (idx >= start) & (idx < start + n)
            p_ = (idx - start) & (per_seq - 1)
            pos = jnp.where(inside, p_, pos)
            left = jnp.where(inside, per_seq - 1 - p_, left)
            start += n
        max_seq = max(per_seq for _, per_seq in seqs)

        def scan(h, col, behind, shift_sign):
            k, i = 1, 0
            while k < max_seq:
                nb = pltpu.roll(h, (shift_sign * k) % nc, axis=0)
                swapped = pltpu.roll(nb, S5_STATE, axis=1)
                a = step_ref[0, i:i + 1, col:col + STATE_LANES]
                b = step_ref[0, 6 + i:7 + i, col:col + STATE_LANES]
                h = h + jnp.where(behind >= k, a * nb + b * swapped, 0.0)
                k, i = 2 * k, i + 1
            carried = pltpu.roll(h, shift_sign % nc, axis=0)
            return jnp.where(behind >= 1, carried, 0.0)

        h_prev = scan(summ[:, :STATE_LANES], 0, pos, 1)
        h_next = scan(summ[:, STATE_LANES:], STATE_LANES, left, -1)
        y = y + _dot(jnp.concatenate([h_prev, h_next], axis=1).astype(BF16), wc_ref[0])

        for p in range(S5_GROUP):
            planes[p * pitch:p * pitch + nc, :] = y[:, p * LANES:(p + 1) * LANES]
        for r8 in range(S5_GROUP // SUBLANES):
            for row, (ref, c) in enumerate(tiles(yt_refs)):
                ref[r8 * SUBLANES:(r8 + 1) * SUBLANES, c * LANES:(c + 1) * LANES] = (
                    planes[pl.ds(r8 * SUBLANES * pitch + row, SUBLANES, stride=pitch), :])

    return kernel, nc, pitch


def _s5_chunk(uts, seq_lens, tables):
    seqs = []
    for ut, l in zip(uts, seq_lens):
        per_seq = l // CHUNK
        assert l % CHUNK == 0 and per_seq & (per_seq - 1) == 0 and ut.shape[1] % l == 0
        seqs.append((ut.shape[1] // CHUNK, per_seq))
    kernel, nc, pitch = _make_s5_kernel(tuple(seqs))
    cb, pw, ws, wc, step = tables
    per_group = lambda a: pl.BlockSpec((1,) + a.shape[1:], lambda g: (g, 0, 0))
    io = [pl.BlockSpec((S5_GROUP, ut.shape[1]), lambda g: (g, 0)) for ut in uts]
    return pl.pallas_call(
        kernel,
        grid=(S5_GROUPS,),
        in_specs=io + [per_group(t) for t in (cb, pw, ws, wc, step)],
        out_specs=io,
        out_shape=[jax.ShapeDtypeStruct(ut.shape, F32) for ut in uts],
        scratch_shapes=[pltpu.VMEM((S5_GROUP * pitch, LANES), F32), pltpu.VMEM((nc, GK), BF16),
                        pltpu.VMEM((GK, GK), BF16), pltpu.VMEM((S5_GROUP * S5_GROUP, 2 * CHUNK), F32)],
        compiler_params=pltpu.CompilerParams(dimension_semantics=("parallel",), vmem_limit_bytes=VMEM_LIMIT),
        name="s5_chunk",
    )(*uts, cb, pw, ws, wc, step)


def _mix_attn_kernel(x_ref, yt_ref, ut_ref, z_ref, zprev_ref, znext_ref, gb_ref, gates_ref,
                     kt_ref, v_ref, d_ref, wglu_ref, cw_ref, wco_ref, wo_ref, gx_ref, wq_ref,
                     wxo_ref, h_ref):
    i = pl.program_id(1)
    n_i = pl.num_programs(1)

    yt = _gelu_tanh(yt_ref[...] + d_ref[...] * ut_ref[...]).astype(BF16)
    glu = lax.dot_general(yt, wglu_ref[...], (((0,), (0,)), ((), ())), preferred_element_type=F32)
    s5_out = glu[:, :D_MODEL] * _sigmoid(glu[:, D_MODEL:])

    z = z_ref[0].astype(F32)
    rows = lax.broadcasted_iota(jnp.int32, z.shape, 0)
    z_prev = jnp.where(i > 0, zprev_ref[0, 15:16, :].astype(F32), 0.0)
    z_next = jnp.where(i < n_i - 1, znext_ref[0, 0:1, :].astype(F32), 0.0)
    z_m1 = jnp.where(rows == 0, z_prev, pltpu.roll(z, 1, axis=0))
    z_p1 = jnp.where(rows == TM - 1, z_next, pltpu.roll(z, TM - 1, axis=0))
    zc = cw_ref[0:1, :] * z_m1 + cw_ref[1:2, :] * z + cw_ref[2:3, :] * z_p1
    conv_out = _dot((gb_ref[0].astype(F32) * zc).astype(BF16), wco_ref[...])

    merged = (gates_ref[0, :, :D_MODEL].astype(F32) * s5_out
              + gates_ref[0, :, D_MODEL:].astype(F32) * conv_out)
    h = x_ref[0] + _dot(merged.astype(BF16), wo_ref[...])

    hn = _rms(h, gx_ref[...]).astype(BF16)
    q = _dot(hn, wq_ref[...]).astype(BF16)
    heads = []
    for hd in range(XATTN_HEADS):
        hs = slice(hd * XATTN_HEAD_DIM, (hd + 1) * XATTN_HEAD_DIM)
        s = _dot(q[:, hs], kt_ref[0, hs, :]) * (XATTN_HEAD_DIM ** -0.5)
        e = jnp.exp(s - jnp.max(s, axis=-1, keepdims=True))
        p = (e / jnp.sum(e, axis=-1, keepdims=True)).astype(BF16)
        heads.append(_dot(p, v_ref[0, :, hs]).astype(BF16))
    o = jnp.concatenate(heads, axis=-1)
    h_ref[0] = h + _dot(o, wxo_ref[...])


def _mix_attn(x, yt, ut, z, gb, gates, kt, v, s5_d, w_glu, conv_w, w_conv_out, w_o, norm_xattn_g, w_q, w_xo):
    b, l, _ = x.shape
    n_i = l // TM
    tile = lambda w: pl.BlockSpec((1, TM, w), lambda bi, i: (bi, i, 0))
    chan = pl.BlockSpec((D_S5, TM), lambda bi, i: (0, bi * n_i + i))
    halo = TM // 16
    z_prev = pl.BlockSpec((1, 16, D_CONV), lambda bi, i: (bi, jnp.maximum(i * halo - 1, 0), 0))
    z_next = pl.BlockSpec((1, 16, D_CONV), lambda bi, i: (bi, jnp.minimum((i + 1) * halo, l // 16 - 1), 0))
    per_batch = lambda r, c: pl.BlockSpec((1, r, c), lambda bi, i: (bi, 0, 0))
    return pl.pallas_call(
        _mix_attn_kernel,
        grid=(b, n_i),
        in_specs=[tile(D_MODEL), chan, chan, tile(D_CONV), z_prev, z_next,
                  tile(D_CONV), tile(2 * D_MODEL), per_batch(D_MODEL, N_MEM), per_batch(N_MEM, D_MODEL),
                  _const_spec((D_S5, 1)), _const_spec((D_S5, 2 * D_MODEL)),
                  _const_spec((3, D_CONV)), _const_spec((D_CONV, D_MODEL)), _const_spec((D_MODEL, D_MODEL)),
                  _const_spec((1, D_MODEL)), _const_spec((D_MODEL, D_MODEL)), _const_spec((D_MODEL, D_MODEL))],
        out_specs=tile(D_MODEL),
        out_shape=jax.ShapeDtypeStruct(x.shape, F32),
        compiler_params=pltpu.CompilerParams(dimension_semantics=("parallel", "parallel"),
                                             vmem_limit_bytes=VMEM_LIMIT),
        name="mix_attn",
    )(x, yt, ut, z, z, z, gb, gates, kt, v, s5_d, w_glu, conv_w, w_conv_out, w_o, norm_xattn_g, w_q, w_xo)


def _ffn_kernel(h_ref, gf_ref, wgu_ref, wd_ref, gout_ref, o_ref):
    h = h_ref[...]
    hn = _rms(h, gf_ref[...]).astype(BF16)
    acc = h
    for c in range(D_FF // FF_CHUNK):
        gate = _dot(hn, wgu_ref[:, c * FF_CHUNK:(c + 1) * FF_CHUNK])
        up = _dot(hn, wgu_ref[:, D_FF + c * FF_CHUNK:D_FF + (c + 1) * FF_CHUNK])
        act = (gate * _sigmoid(gate) * up).astype(BF16)
        acc = acc + _dot(act, wd_ref[c * FF_CHUNK:(c + 1) * FF_CHUNK, :])
    o_ref[...] = _rms(acc, gout_ref[...])


def _ffn_out(h2d, norm_ffn_g, w_gate_up, w_down, norm_final_g):
    n = h2d.shape[0]
    row = pl.BlockSpec((TM, D_MODEL), lambda i: (i, 0))
    return pl.pallas_call(
        _ffn_kernel,
        grid=(n // TM,),
        in_specs=[row, _const_spec((1, D_MODEL)), _const_spec((D_MODEL, 2 * D_FF)),
                  _const_spec((D_FF, D_MODEL)), _const_spec((1, D_MODEL))],
        out_specs=row,
        out_shape=jax.ShapeDtypeStruct(h2d.shape, F32),
        compiler_params=pltpu.CompilerParams(dimension_semantics=("parallel",), vmem_limit_bytes=VMEM_LIMIT),
        name="ffn_out",
    )(h2d, norm_ffn_g, w_gate_up, w_down, norm_final_g)


def _front(x, mem, w):
    b, l, _ = x.shape
    ut, z, gb, gates = _in_proj(x.reshape(b * l, D_MODEL), w["norm_mix_g"], w["w_ut"], w["w_in_rest"], w["b_gate"])
    k, v = _mem_kv(mem, w["norm_mem_g"], w["w_kv"])
    shp = lambda a: a.reshape(b, l, a.shape[-1])
    return ut, shp(z), shp(gb), shp(gates), jnp.swapaxes(k, 1, 2), v


def _back(x, yt, front, w):
    ut, z, gb, gates, kt, v = front
    b, l, _ = x.shape
    h = _mix_attn(x, yt, ut, z, gb, gates, kt, v, w["s5_d"], w["w_glu"], w["conv_w"], w["w_conv_out"], w["w_o"],
                  w["norm_xattn_g"], w["w_q"], w["w_xo"])
    out = _ffn_out(h.reshape(b * l, D_MODEL), w["norm_ffn_g"], w["w_gate_up"], w["w_down"], w["norm_final_g"])
    return out.reshape(b, l, D_MODEL)


def kernel(x_prompt, x_sample, mem_prompt, mem_sample, norm_mix_g, w_in, b_gate, s5_a_re, s5_a_im, s5_log_dt, s5_b_re, s5_b_im, s5_c_re, s5_c_im, s5_d, w_glu_a, w_glu_b, conv_w, w_conv_out, w_o, norm_xattn_g, norm_mem_g, w_q, w_kv, w_xo, norm_ffn_g, w_gate_up, w_down, norm_final_g):
    row = lambda a: a.reshape(1, -1).astype(F32)
    w = {
        "norm_mix_g": row(norm_mix_g[0]), "w_ut": w_in[0, :, :D_S5].T.astype(BF16),
        "w_in_rest": w_in[0, :, D_S5:].astype(BF16), "b_gate": row(b_gate[0]),
        "s5_d": s5_d[0].astype(F32).reshape(D_S5, 1),
        "w_glu": jnp.concatenate([w_glu_a[0], w_glu_b[0]], axis=1).astype(BF16),
        "conv_w": conv_w[0].astype(F32), "w_conv_out": w_conv_out[0].astype(BF16), "w_o": w_o[0].astype(BF16),
        "norm_xattn_g": row(norm_xattn_g[0]), "norm_mem_g": row(norm_mem_g[0]),
        "w_q": w_q[0].astype(BF16), "w_kv": w_kv[0].astype(BF16), "w_xo": w_xo[0].astype(BF16),
        "norm_ffn_g": row(norm_ffn_g[0]), "w_gate_up": w_gate_up[0].astype(BF16),
        "w_down": w_down[0].astype(BF16), "norm_final_g": row(norm_final_g),
    }
    tables = _s5_tables(s5_a_re[0], s5_a_im[0], s5_log_dt[0], s5_b_re[0], s5_b_im[0], s5_c_re[0], s5_c_im[0])
    fp, fs = _front(x_prompt, mem_prompt, w), _front(x_sample, mem_sample, w)
    ytp, yts = _s5_chunk([fp[0], fs[0]], [x_prompt.shape[1], x_sample.shape[1]], tables)
    return _back(x_prompt, ytp, fp, w), _back(x_sample, yts, fs, w)
```

```python
import math

import jax
import jax.numpy as jnp
from jax import lax
from jax.experimental import pallas as pl
from jax.experimental.pallas import tpu as pltpu

F32 = jnp.float32
BF16 = jnp.bfloat16

D_MODEL = 1024
D_S5 = D_MODEL // 2
S5_GROUP = 16
S5_GROUPS = D_S5 // S5_GROUP
S5_STATE = 64
D_CONV = D_MODEL // 2
N_MEM = 256
XATTN_HEADS = 4
XATTN_HEAD_DIM = D_MODEL // XATTN_HEADS
D_FF = -(-8 * D_MODEL // (3 * 256)) * 256
EPS = 1e-6

LANES = 128
SUBLANES = 8
VMEM_LIMIT = 56 * 1024 * 1024

MXU_DIM = 256
TM = 1024
CHUNK = LANES
STATE_LANES = 2 * S5_STATE
GK = S5_GROUP * CHUNK
FF_CHUNK = 3 * MXU_DIM


def _const_spec(shape):
    zeros = (0,) * len(shape)
    return pl.BlockSpec(shape, lambda *_: zeros, pipeline_mode=pl.Buffered(1))


def _rms(x, g):
    ms = jnp.mean(x * x, axis=-1, keepdims=True)
    return x * lax.rsqrt(ms + EPS) * g


def _sigmoid(x):
    return 1.0 / (1.0 + jnp.exp(-x))


def _gelu_tanh(x):
    c = math.sqrt(2.0 / math.pi)
    return 0.5 * x * (1.0 + jnp.tanh(c * (x + 0.044715 * (x * x * x))))


def _dot(a, b):
    return jnp.dot(a, b, preferred_element_type=F32)


def _in_proj_kernel(x_ref, g_ref, wut_ref, w_ref, bg_ref, ut_ref, z_ref, gb_ref, gates_ref):
    xn = _rms(x_ref[...], g_ref[...]).astype(BF16)
    ut_ref[...] = lax.dot_general(wut_ref[...], xn, (((1,), (1,)), ((), ())), preferred_element_type=F32)
    x_c = _dot(xn, w_ref[:, 0:D_CONV])
    gb_ref[...] = _dot(xn, w_ref[:, D_CONV:2 * D_CONV]).astype(BF16)
    gate_c = _dot(xn, w_ref[:, 2 * D_CONV:3 * D_CONV])
    z_ref[...] = (gate_c * x_c).astype(BF16)
    g = _dot(xn, w_ref[:, 3 * D_CONV:])
    gates_ref[...] = _sigmoid(g + bg_ref[...]).astype(BF16)


def _in_proj(x2d, norm_g, w_ut, w_rest, b_gate):
    n = x2d.shape[0]
    row = lambda w: pl.BlockSpec((TM, w), lambda i: (i, 0))
    return pl.pallas_call(
        _in_proj_kernel,
        grid=(n // TM,),
        in_specs=[row(D_MODEL), _const_spec((1, D_MODEL)), _const_spec(w_ut.shape), _const_spec(w_rest.shape),
                  _const_spec((1, 2 * D_MODEL))],
        out_specs=[pl.BlockSpec((D_S5, TM), lambda i: (0, i)), row(D_CONV), row(D_CONV), row(2 * D_MODEL)],
        out_shape=[jax.ShapeDtypeStruct((D_S5, n), F32), jax.ShapeDtypeStruct((n, D_CONV), BF16),
                   jax.ShapeDtypeStruct((n, D_CONV), BF16), jax.ShapeDtypeStruct((n, 2 * D_MODEL), BF16)],
        compiler_params=pltpu.CompilerParams(dimension_semantics=("parallel",), vmem_limit_bytes=VMEM_LIMIT),
        name="in_proj",
    )(x2d, norm_g, w_ut, w_rest, b_gate)


def _mem_kv_kernel(m_ref, g_ref, w_ref, k_ref, v_ref):
    mn = _rms(m_ref[0], g_ref[...]).astype(BF16)
    k_ref[0] = _dot(mn, w_ref[:, :D_MODEL]).astype(BF16)
    v_ref[0] = _dot(mn, w_ref[:, D_MODEL:]).astype(BF16)


def _mem_kv(mem, norm_g, w_kv):
    b = mem.shape[0]
    blk = pl.BlockSpec((1, N_MEM, D_MODEL), lambda i: (i, 0, 0))
    return pl.pallas_call(
        _mem_kv_kernel,
        grid=(b,),
        in_specs=[blk, _const_spec((1, D_MODEL)), _const_spec((D_MODEL, 2 * D_MODEL))],
        out_specs=[blk, blk],
        out_shape=[jax.ShapeDtypeStruct((b, N_MEM, D_MODEL), BF16)] * 2,
        compiler_params=pltpu.CompilerParams(dimension_semantics=("parallel",), vmem_limit_bytes=VMEM_LIMIT),
        name="mem_kv",
    )(mem, norm_g, w_kv)


def _s5_tables(a_re, a_im, log_dt, b_re, b_im, c_re, c_im):
    f = lambda v: v.astype(F32)
    a_re, a_im, b_re, b_im, c_re, c_im = map(f, (a_re, a_im, b_re, b_im, c_re, c_im))
    dt = jnp.exp(f(log_dt))[..., None]
    lr, li = a_re * dt, a_im * dt

    def apow(k):
        m = jnp.exp(lr[..., None] * k)
        return m * jnp.cos(li[..., None] * k), m * jnp.sin(li[..., None] * k)

    ab_re, ab_im = jnp.exp(lr) * jnp.cos(li), jnp.exp(lr) * jnp.sin(li)
    x, y, den = ab_re - 1.0, ab_im, a_re * a_re + a_im * a_im
    cf_re, cf_im = (x * a_re + y * a_im) / den, (y * a_re - x * a_im) / den
    bb_re = cf_re[..., None] * b_re - cf_im[..., None] * b_im
    bb_im = cf_re[..., None] * b_im + cf_im[..., None] * b_re
    bq_re, bq_im = jnp.swapaxes(bb_re, -1, -2), jnp.swapaxes(bb_im, -1, -2)
    g = a_re.shape[1]

    cbr = c_re[:, :, None] * bq_re[:, :, :, None] - c_im[:, :, None] * bq_im[:, :, :, None]
    cbi = c_re[:, :, None] * bq_im[:, :, :, None] + c_im[:, :, None] * bq_re[:, :, :, None]
    half = jnp.concatenate([cbr, -cbi], axis=-1).reshape(2, g, S5_GROUP * S5_GROUP, STATE_LANES)
    cb = jnp.concatenate([half[0], half[1]], axis=-1)
    j = jnp.arange(2 * CHUNK, dtype=F32)
    pf_re, pf_im = apow(jnp.maximum(j - CHUNK, 0.0))
    pb_re, pb_im = apow(jnp.maximum(CHUNK - j, 0.0))
    mf, mb = (j >= CHUNK).astype(F32), (j <= CHUNK).astype(F32)
    pw = jnp.concatenate([pf_re[0] * mf, pf_im[0] * mf, pb_re[1] * mb, pb_im[1] * mb], axis=1)

    s = jnp.arange(CHUNK, dtype=F32)
    in_re, in_im = apow(jnp.stack([CHUNK - 1.0 - s, s])[:, None, None, :])
    out_re, out_im = apow(jnp.stack([s + 1.0, CHUNK - s])[:, None, None, :])
    both = lambda re, im: jnp.concatenate([re[0], im[0], re[1], im[1]], axis=-1)
    tr = lambda v: jnp.swapaxes(v, -1, -2)
    gen = jnp.concatenate([both(bq_re, bq_re), both(-bq_im, bq_im), both(c_re, -c_re), both(-c_im, -c_im),
                           both(tr(in_re), tr(in_im)), both(tr(out_re), tr(out_im))], axis=1)

    ks = CHUNK * (2.0 ** jnp.arange(6, dtype=F32))
    t_re, t_im = apow(ks)
    t_re, t_im = jnp.moveaxis(t_re, -1, 2), jnp.moveaxis(t_im, -1, 2)
    row_a = jnp.concatenate([t_re[0], t_re[0], t_re[1], t_re[1]], axis=-1)
    row_b = jnp.concatenate([-t_im[0], t_im[0], -t_im[1], t_im[1]], axis=-1)
    step = jnp.concatenate([row_a, row_b], axis=1)
    return cb, pw, gen, step


def _chunk_pitch(nc):
    p = -(-nc // SUBLANES) * SUBLANES
    return p if (p // SUBLANES) % 2 == 1 else p + SUBLANES


def _make_s5_kernel(seqs):
    nc = sum(n for n, _ in seqs)
    pitch = _chunk_pitch(nc)

    def kernel(*refs):
        n_in = len(seqs)
        ut_refs = refs[:n_in]
        cb_ref, pw_ref, gen_ref, step_ref = refs[n_in:n_in + 4]
        yt_refs = refs[n_in + 4:2 * n_in + 4]
        planes, xcat, toep, lag, ws, wct = refs[2 * n_in + 4:]

        def tiles(refs_):
            for ref, (n, _) in zip(refs_, seqs):
                for c in range(n):
                    yield ref, c

        for r8 in range(S5_GROUP // SUBLANES):
            for row, (ref, c) in enumerate(tiles(ut_refs)):
                planes[pl.ds(r8 * SUBLANES * pitch + row, SUBLANES, stride=pitch), :] = (
                    ref[r8 * SUBLANES:(r8 + 1) * SUBLANES, c * LANES:(c + 1) * LANES])
        for q in range(S5_GROUP):
            xcat[:, q * LANES:(q + 1) * LANES] = planes[q * pitch:q * pitch + nc, :].astype(BF16)

        lag[...] = jnp.dot(cb_ref[0], pw_ref[0], preferred_element_type=F32, precision=lax.Precision.HIGHEST)

        def toeplitz_rows(q, carry):
            for p in range(S5_GROUP):
                row = lag[pl.ds(q * S5_GROUP + p, 1), :]
                skew = pltpu.roll(jnp.broadcast_to(row, (CHUNK, 2 * CHUNK)), 0, axis=1, stride=1, stride_axis=0)
                toep[pl.ds(pl.multiple_of(q * CHUNK, CHUNK), CHUNK), p * LANES:(p + 1) * LANES] = (
                    skew[:, CHUNK:].astype(BF16))
            return carry

        lax.fori_loop(0, S5_GROUP, toeplitz_rows, 0)

        def outer(dst, f1, f2, powers):
            swapped = jnp.concatenate([pltpu.roll(powers[:, :STATE_LANES], S5_STATE, axis=1),
                                       pltpu.roll(powers[:, STATE_LANES:], S5_STATE, axis=1)], axis=1)
            for ch in range(S5_GROUP):
                blk = gen_ref[0, f1 + ch:f1 + ch + 1, :] * powers + gen_ref[0, f2 + ch:f2 + ch + 1, :] * swapped
                dst[ch * CHUNK:(ch + 1) * CHUNK, :] = blk.astype(BF16)

        outer(ws, 0, S5_GROUP, gen_ref[0, 4 * S5_GROUP:4 * S5_GROUP + CHUNK, :])
        outer(wct, 2 * S5_GROUP, 3 * S5_GROUP, gen_ref[0, 4 * S5_GROUP + CHUNK:, :])

        x = xcat[...]
        y = _dot(x, toep[...])
        summ = _dot(x, ws[...])

        idx = lax.broadcasted_iota(jnp.int32, (nc, LANES), 0)
        pos, left, start = idx, idx, 0
        for n, per_seq in seqs:
            inside = (idx >= start) & (idx < start + n)
            p_ = (idx - start) & (per_seq - 1)
            pos = jnp.where(inside, p_, pos)
            left = jnp.where(inside, per_seq - 1 - p_, left)
            start += n
        max_seq = max(per_seq for _, per_seq in seqs)

        def scan(h, col, behind, shift_sign):
            k, i = 1, 0
            while k < max_seq:
                nb = pltpu.roll(h, (shift_sign * k) % nc, axis=0)
                swapped = pltpu.roll(nb, S5_STATE, axis=1)
                a = step_ref[0, i:i + 1, col:col + STATE_LANES]
                b = step_ref[0, 6 + i:7 + i, col:col + STATE_LANES]
                h = h + jnp.where(behind >= k, a * nb + b * swapped, 0.0)
                k, i = 2 * k, i + 1
            carried = pltpu.roll(h, shift_sign % nc, axis=0)
            return jnp.where(behind >= 1, carried, 0.0)

        h_prev = scan(summ[:, :STATE_LANES], 0, pos, 1)
        h_next = scan(summ[:, STATE_LANES:], STATE_LANES, left, -1)
        carried = jnp.concatenate([h_prev, h_next], axis=1).astype(BF16)
        y = y + lax.dot_general(carried, wct[...], (((1,), (1,)), ((), ())), preferred_element_type=F32)

        for p in range(S5_GROUP):
            planes[p * pitch:p * pitch + nc, :] = y[:, p * LANES:(p + 1) * LANES]
        for r8 in range(S5_GROUP // SUBLANES):
            for row, (ref, c) in enumerate(tiles(yt_refs)):
                ref[r8 * SUBLANES:(r8 + 1) * SUBLANES, c * LANES:(c + 1) * LANES] = (
                    planes[pl.ds(r8 * SUBLANES * pitch + row, SUBLANES, stride=pitch), :])

    return kernel, nc, pitch


def _s5_chunk(uts, seq_lens, tables):
    seqs = []
    for ut, l in zip(uts, seq_lens):
        per_seq = l // CHUNK
        assert l % CHUNK == 0 and per_seq & (per_seq - 1) == 0 and ut.shape[1] % l == 0
        seqs.append((ut.shape[1] // CHUNK, per_seq))
    kernel, nc, pitch = _make_s5_kernel(tuple(seqs))
    per_group = lambda a: pl.BlockSpec((1,) + a.shape[1:], lambda g: (g, 0, 0))
    io = [pl.BlockSpec((S5_GROUP, ut.shape[1]), lambda g: (g, 0)) for ut in uts]
    return pl.pallas_call(
        kernel,
        grid=(S5_GROUPS,),
        in_specs=io + [per_group(t) for t in tables],
        out_specs=io,
        out_shape=[jax.ShapeDtypeStruct(ut.shape, F32) for ut in uts],
        scratch_shapes=[pltpu.VMEM((S5_GROUP * pitch, LANES), F32), pltpu.VMEM((nc, GK), BF16),
                        pltpu.VMEM((GK, GK), BF16), pltpu.VMEM((S5_GROUP * S5_GROUP, 2 * CHUNK), F32),
                        pltpu.VMEM((GK, 2 * STATE_LANES), BF16), pltpu.VMEM((GK, 2 * STATE_LANES), BF16)],
        compiler_params=pltpu.CompilerParams(dimension_semantics=("parallel",), vmem_limit_bytes=VMEM_LIMIT),
        name="s5_chunk",
    )(*uts, *tables)


def _mix_attn_kernel(x_ref, yt_ref, ut_ref, z_ref, zprev_ref, znext_ref, gb_ref, gates_ref,
                     kt_ref, v_ref, d_ref, wglu_ref, cw_ref, wco_ref, wo_ref, gx_ref, wq_ref,
                     wxo_ref, h_ref):
    i = pl.program_id(1)
    n_i = pl.num_programs(1)

    yt = _gelu_tanh(yt_ref[...] + d_ref[...] * ut_ref[...]).astype(BF16)
    glu = lax.dot_general(yt, wglu_ref[...], (((0,), (0,)), ((), ())), preferred_element_type=F32)
    s5_out = glu[:, :D_MODEL] * _sigmoid(glu[:, D_MODEL:])

    z = z_ref[0].astype(F32)
    rows = lax.broadcasted_iota(jnp.int32, z.shape, 0)
    z_prev = jnp.where(i > 0, zprev_ref[0, 15:16, :].astype(F32), 0.0)
    z_next = jnp.where(i < n_i - 1, znext_ref[0, 0:1, :].astype(F32), 0.0)
    z_m1 = jnp.where(rows == 0, z_prev, pltpu.roll(z, 1, axis=0))
    z_p1 = jnp.where(rows == TM - 1, z_next, pltpu.roll(z, TM - 1, axis=0))
    zc = cw_ref[0:1, :] * z_m1 + cw_ref[1:2, :] * z + cw_ref[2:3, :] * z_p1
    conv_out = _dot((gb_ref[0].astype(F32) * zc).astype(BF16), wco_ref[...])

    merged = (gates_ref[0, :, :D_MODEL].astype(F32) * s5_out
              + gates_ref[0, :, D_MODEL:].astype(F32) * conv_out)
    h = x_ref[0] + _dot(merged.astype(BF16), wo_ref[...])

    hn = _rms(h, gx_ref[...]).astype(BF16)
    q = _dot(hn, wq_ref[...]).astype(BF16)
    heads = []
    for hd in range(XATTN_HEADS):
        hs = slice(hd * XATTN_HEAD_DIM, (hd + 1) * XATTN_HEAD_DIM)
        s = _dot(q[:, hs], kt_ref[0, hs, :]) * (XATTN_HEAD_DIM ** -0.5)
        e = jnp.exp(s - jnp.max(s, axis=-1, keepdims=True))
        p = (e / jnp.sum(e, axis=-1, keepdims=True)).astype(BF16)
        heads.append(_dot(p, v_ref[0, :, hs]).astype(BF16))
    o = jnp.concatenate(heads, axis=-1)
    h_ref[0] = h + _dot(o, wxo_ref[...])


def _mix_attn(x, yt, ut, z, gb, gates, kt, v, s5_d, w_glu, conv_w, w_conv_out, w_o, norm_xattn_g, w_q, w_xo):
    b, l, _ = x.shape
    n_i = l // TM
    tile = lambda w: pl.BlockSpec((1, TM, w), lambda bi, i: (bi, i, 0))
    chan = pl.BlockSpec((D_S5, TM), lambda bi, i: (0, bi * n_i + i))
    halo = TM // 16
    z_prev = pl.BlockSpec((1, 16, D_CONV), lambda bi, i: (bi, jnp.maximum(i * halo - 1, 0), 0))
    z_next = pl.BlockSpec((1, 16, D_CONV), lambda bi, i: (bi, jnp.minimum((i + 1) * halo, l // 16 - 1), 0))
    per_batch = lambda r, c: pl.BlockSpec((1, r, c), lambda bi, i: (bi, 0, 0))
    return pl.pallas_call(
        _mix_attn_kernel,
        grid=(b, n_i),
        in_specs=[tile(D_MODEL), chan, chan, tile(D_CONV), z_prev, z_next,
                  tile(D_CONV), tile(2 * D_MODEL), per_batch(D_MODEL, N_MEM), per_batch(N_MEM, D_MODEL),
                  _const_spec((D_S5, 1)), _const_spec((D_S5, 2 * D_MODEL)),
                  _const_spec((3, D_CONV)), _const_spec((D_CONV, D_MODEL)), _const_spec((D_MODEL, D_MODEL)),
                  _const_spec((1, D_MODEL)), _const_spec((D_MODEL, D_MODEL)), _const_spec((D_MODEL, D_MODEL))],
        out_specs=tile(D_MODEL),
        out_shape=jax.ShapeDtypeStruct(x.shape, F32),
        compiler_params=pltpu.CompilerParams(dimension_semantics=("parallel", "parallel"),
                                             vmem_limit_bytes=VMEM_LIMIT),
        name="mix_attn",
    )(x, yt, ut, z, z, z, gb, gates, kt, v, s5_d, w_glu, conv_w, w_conv_out, w_o, norm_xattn_g, w_q, w_xo)


def _ffn_kernel(h_ref, gf_ref, wgu_ref, wd_ref, gout_ref, o_ref):
    h = h_ref[...]
    hn = _rms(h, gf_ref[...]).astype(BF16)
    acc = h
    for lo in range(0, D_FF, FF_CHUNK):
        hi = min(lo + FF_CHUNK, D_FF)
        gate = _dot(hn, wgu_ref[:, lo:hi])
        up = _dot(hn, wgu_ref[:, D_FF + lo:D_FF + hi])
        act = (gate * _sigmoid(gate) * up).astype(BF16)
        acc = acc + _dot(act, wd_ref[lo:hi, :])
    o_ref[...] = _rms(acc, gout_ref[...])


def _ffn_out(h2d, norm_ffn_g, w_gate_up, w_down, norm_final_g):
    n = h2d.shape[0]
    row = pl.BlockSpec((TM, D_MODEL), lambda i: (i, 0))
    return pl.pallas_call(
        _ffn_kernel,
        grid=(n // TM,),
        in_specs=[row, _const_spec((1, D_MODEL)), _const_spec((D_MODEL, 2 * D_FF)),
                  _const_spec((D_FF, D_MODEL)), _const_spec((1, D_MODEL))],
        out_specs=row,
        out_shape=jax.ShapeDtypeStruct(h2d.shape, F32),
        compiler_params=pltpu.CompilerParams(dimension_semantics=("parallel",), vmem_limit_bytes=VMEM_LIMIT),
        name="ffn_out",
    )(h2d, norm_ffn_g, w_gate_up, w_down, norm_final_g)


def _front(x, mem, w):
    b, l, _ = x.shape
    ut, z, gb, gates = _in_proj(x.reshape(b * l, D_MODEL), w["norm_mix_g"], w["w_ut"], w["w_in_rest"], w["b_gate"])
    k, v = _mem_kv(mem, w["norm_mem_g"], w["w_kv"])
    shp = lambda a: a.reshape(b, l, a.shape[-1])
    return ut, shp(z), shp(gb), shp(gates), jnp.swapaxes(k, 1, 2), v


def _back(x, yt, front, w):
    ut, z, gb, gates, kt, v = front
    b, l, _ = x.shape
    h = _mix_attn(x, yt, ut, z, gb, gates, kt, v, w["s5_d"], w["w_glu"], w["conv_w"], w["w_conv_out"], w["w_o"],
                  w["norm_xattn_g"], w["w_q"], w["w_xo"])
    out = _ffn_out(h.reshape(b * l, D_MODEL), w["norm_ffn_g"], w["w_gate_up"], w["w_down"], w["norm_final_g"])
    return out.reshape(b, l, D_MODEL)


def kernel(x_prompt, x_sample, mem_prompt, mem_sample, norm_mix_g, w_in, b_gate, s5_a_re, s5_a_im, s5_log_dt, s5_b_re, s5_b_im, s5_c_re, s5_c_im, s5_d, w_glu_a, w_glu_b, conv_w, w_conv_out, w_o, norm_xattn_g, norm_mem_g, w_q, w_kv, w_xo, norm_ffn_g, w_gate_up, w_down, norm_final_g):
    row = lambda a: a.reshape(1, -1).astype(F32)
    w = {
        "norm_mix_g": row(norm_mix_g[0]), "w_ut": w_in[0, :, :D_S5].T.astype(BF16),
        "w_in_rest": w_in[0, :, D_S5:].astype(BF16), "b_gate": row(b_gate[0]),
        "s5_d": s5_d[0].astype(F32).reshape(D_S5, 1),
        "w_glu": jnp.concatenate([w_glu_a[0], w_glu_b[0]], axis=1).astype(BF16),
        "conv_w": conv_w[0].astype(F32), "w_conv_out": w_conv_out[0].astype(BF16), "w_o": w_o[0].astype(BF16),
        "norm_xattn_g": row(norm_xattn_g[0]), "norm_mem_g": row(norm_mem_g[0]),
        "w_q": w_q[0].astype(BF16), "w_kv": w_kv[0].astype(BF16), "w_xo": w_xo[0].astype(BF16),
        "norm_ffn_g": row(norm_ffn_g[0]), "w_gate_up": w_gate_up[0].astype(BF16),
        "w_down": w_down[0].astype(BF16), "norm_final_g": row(norm_final_g),
    }
    tables = _s5_tables(s5_a_re[0], s5_a_im[0], s5_log_dt[0], s5_b_re[0], s5_b_im[0], s5_c_re[0], s5_c_im[0])
    fp, fs = _front(x_prompt, mem_prompt, w), _front(x_sample, mem_sample, w)
    ytp, yts = _s5_chunk([fp[0], fs[0]], [x_prompt.shape[1], x_sample.shape[1]], tables)
    return _back(x_prompt, ytp, fp, w), _back(x_sample, yts, fs, w)
```

```python
import math

import jax
import jax.numpy as jnp
from jax import lax
from jax.experimental import pallas as pl
from jax.experimental.pallas import tpu as pltpu

F32 = jnp.float32
BF16 = jnp.bfloat16

D_MODEL = 1024
D_S5 = D_MODEL // 2
S5_GROUP = 16
S5_GROUPS = D_S5 // S5_GROUP
S5_STATE = 64
D_CONV = D_MODEL // 2
N_MEM = 256
XATTN_HEADS = 4
XATTN_HEAD_DIM = D_MODEL // XATTN_HEADS
D_FF = -(-8 * D_MODEL // (3 * 256)) * 256
EPS = 1e-6

LANES = 128
SUBLANES = 8
VMEM_LIMIT = 56 * 1024 * 1024

MXU_DIM = 256
TM = 1024
CHUNK = LANES
TILE_CHUNKS = TM // CHUNK
STATE_LANES = 2 * S5_STATE
GK = S5_GROUP * CHUNK
FF_CHUNK = 3 * MXU_DIM


def _const_spec(shape):
    zeros = (0,) * len(shape)
    return pl.BlockSpec(shape, lambda *_: zeros, pipeline_mode=pl.Buffered(1))


def _rms(x, g):
    ms = jnp.mean(x * x, axis=-1, keepdims=True)
    return x * lax.rsqrt(ms + EPS) * g


def _sigmoid(x):
    return 1.0 / (1.0 + jnp.exp(-x))


def _gelu_tanh(x):
    c = math.sqrt(2.0 / math.pi)
    return x * (0.5 + 0.5 * jnp.tanh(x * (c + (c * 0.044715) * (x * x))))


def _dot(a, b):
    return jnp.dot(a, b, preferred_element_type=F32)


def _planes_view(ref):
    assert ref.shape[1] == TILE_CHUNKS and ref.shape[2] == LANES
    return ref.reshape(ref.shape[0] * TILE_CHUNKS, LANES)


def _plane_rows(r8, c):
    return pl.ds(r8 * SUBLANES * TILE_CHUNKS + c, SUBLANES, stride=TILE_CHUNKS)


def _in_proj_kernel(x_ref, g_ref, wut_ref, w_ref, bg_ref, up_ref, z_ref, gb_ref, gates_ref):
    xn = _rms(x_ref[...], g_ref[...]).astype(BF16)
    g = _dot(xn, w_ref[:, 3 * D_CONV:])
    gates_ref[...] = _sigmoid(g + bg_ref[...]).astype(BF16)
    ut = lax.dot_general(wut_ref[...], xn, (((1,), (1,)), ((), ())), preferred_element_type=F32)
    planes = _planes_view(up_ref)
    for r8 in range(D_S5 // SUBLANES):
        for c in range(TILE_CHUNKS):
            planes[_plane_rows(r8, c), :] = ut[r8 * SUBLANES:(r8 + 1) * SUBLANES, c * LANES:(c + 1) * LANES]
    x_c = _dot(xn, w_ref[:, 0:D_CONV])
    gate_c = _dot(xn, w_ref[:, 2 * D_CONV:3 * D_CONV])
    z_ref[...] = (gate_c * x_c).astype(BF16)
    gb_ref[...] = _dot(xn, w_ref[:, D_CONV:2 * D_CONV]).astype(BF16)


def _in_proj(x2d, norm_g, w_ut, w_rest, b_gate):
    n = x2d.shape[0]
    row = lambda w: pl.BlockSpec((TM, w), lambda i: (i, 0))
    return pl.pallas_call(
        _in_proj_kernel,
        grid=(n // TM,),
        in_specs=[row(D_MODEL), _const_spec((1, D_MODEL)), _const_spec(w_ut.shape), _const_spec(w_rest.shape),
                  _const_spec((1, 2 * D_MODEL))],
        out_specs=[pl.BlockSpec((D_S5, TILE_CHUNKS, LANES), lambda i: (0, i, 0)), row(D_CONV), row(D_CONV),
                   row(2 * D_MODEL)],
        out_shape=[jax.ShapeDtypeStruct((D_S5, n // CHUNK, LANES), F32), jax.ShapeDtypeStruct((n, D_CONV), BF16),
                   jax.ShapeDtypeStruct((n, D_CONV), BF16), jax.ShapeDtypeStruct((n, 2 * D_MODEL), BF16)],
        compiler_params=pltpu.CompilerParams(dimension_semantics=("parallel",), vmem_limit_bytes=VMEM_LIMIT),
        name="in_proj",
    )(x2d, norm_g, w_ut, w_rest, b_gate)


def _mem_kv_kernel(m_ref, g_ref, w_ref, k_ref, v_ref):
    mn = _rms(m_ref[0], g_ref[...]).astype(BF16)
    k_ref[0] = _dot(mn, w_ref[:, :D_MODEL]).astype(BF16)
    v_ref[0] = _dot(mn, w_ref[:, D_MODEL:]).astype(BF16)


def _mem_kv(mem, norm_g, w_kv):
    b = mem.shape[0]
    blk = pl.BlockSpec((1, N_MEM, D_MODEL), lambda i: (i, 0, 0))
    return pl.pallas_call(
        _mem_kv_kernel,
        grid=(b,),
        in_specs=[blk, _const_spec((1, D_MODEL)), _const_spec((D_MODEL, 2 * D_MODEL))],
        out_specs=[blk, blk],
        out_shape=[jax.ShapeDtypeStruct((b, N_MEM, D_MODEL), BF16)] * 2,
        compiler_params=pltpu.CompilerParams(dimension_semantics=("parallel",), vmem_limit_bytes=VMEM_LIMIT),
        name="mem_kv",
    )(mem, norm_g, w_kv)


def _s5_tables(a_re, a_im, log_dt, b_re, b_im, c_re, c_im):
    f = lambda v: v.astype(F32)
    a_re, a_im, b_re, b_im, c_re, c_im = map(f, (a_re, a_im, b_re, b_im, c_re, c_im))
    dt = jnp.exp(f(log_dt))[..., None]
    lr, li = a_re * dt, a_im * dt

    def apow(k):
        m = jnp.exp(lr[..., None] * k)
        return m * jnp.cos(li[..., None] * k), m * jnp.sin(li[..., None] * k)

    ab_re, ab_im = jnp.exp(lr) * jnp.cos(li), jnp.exp(lr) * jnp.sin(li)
    x, y, den = ab_re - 1.0, ab_im, a_re * a_re + a_im * a_im
    cf_re, cf_im = (x * a_re + y * a_im) / den, (y * a_re - x * a_im) / den
    bb_re = cf_re[..., None] * b_re - cf_im[..., None] * b_im
    bb_im = cf_re[..., None] * b_im + cf_im[..., None] * b_re
    bq_re, bq_im = jnp.swapaxes(bb_re, -1, -2), jnp.swapaxes(bb_im, -1, -2)
    g = a_re.shape[1]

    cbr = c_re[:, :, None] * bq_re[:, :, :, None] - c_im[:, :, None] * bq_im[:, :, :, None]
    cbi = c_re[:, :, None] * bq_im[:, :, :, None] + c_im[:, :, None] * bq_re[:, :, :, None]
    half = jnp.concatenate([cbr, -cbi], axis=-1).reshape(2, g, S5_GROUP * S5_GROUP, STATE_LANES)
    cb = jnp.concatenate([half[0], half[1]], axis=-1)
    j = jnp.arange(2 * CHUNK, dtype=F32)
    pf_re, pf_im = apow(jnp.maximum(j - CHUNK, 0.0))
    pb_re, pb_im = apow(jnp.maximum(CHUNK - j, 0.0))
    mf, mb = (j >= CHUNK).astype(F32), (j <= CHUNK).astype(F32)
    pw = jnp.concatenate([pf_re[0] * mf, pf_im[0] * mf, pb_re[1] * mb, pb_im[1] * mb], axis=1)

    s = jnp.arange(CHUNK, dtype=F32)
    in_re, in_im = apow(jnp.stack([CHUNK - 1.0 - s, s])[:, None, None, :])
    out_re, out_im = apow(jnp.stack([s + 1.0, CHUNK - s])[:, None, None, :])
    both = lambda re, im: jnp.concatenate([re[0], im[0], re[1], im[1]], axis=-1)
    tr = lambda v: jnp.swapaxes(v, -1, -2)
    gen = jnp.concatenate([both(bq_re, bq_re), both(-bq_im, bq_im), both(c_re, -c_re), both(-c_im, -c_im),
                           both(tr(in_re), tr(in_im)), both(tr(out_re), tr(out_im))], axis=1)

    ks = CHUNK * (2.0 ** jnp.arange(6, dtype=F32))
    t_re, t_im = apow(ks)
    t_re, t_im = jnp.moveaxis(t_re, -1, 2), jnp.moveaxis(t_im, -1, 2)
    row_a = jnp.concatenate([t_re[0], t_re[0], t_re[1], t_re[1]], axis=-1)
    row_b = jnp.concatenate([-t_im[0], t_im[0], -t_im[1], t_im[1]], axis=-1)
    step = jnp.concatenate([row_a, row_b], axis=1)
    return cb, pw, gen, step


def _make_s5_kernel(seqs):
    nc = sum(n for n, _ in seqs)
    starts = [sum(n for n, _ in seqs[:k]) for k in range(len(seqs))]

    def kernel(*refs):
        n_in = len(seqs)
        up_refs = refs[:n_in]
        cb_ref, pw_ref, gen_ref, step_ref, d_ref = refs[n_in:n_in + 5]
        yp_refs = refs[n_in + 5:2 * n_in + 5]
        xcat, toep, lag, ws, wct = refs[2 * n_in + 5:]

        for ref, start, (n, _) in zip(up_refs, starts, seqs):
            for q in range(S5_GROUP):
                xcat[start:start + n, q * LANES:(q + 1) * LANES] = ref[q].astype(BF16)

        lag[...] = jnp.dot(cb_ref[0], pw_ref[0], preferred_element_type=F32, precision=lax.Precision.HIGHEST)

        def toeplitz_rows(q, carry):
            for p in range(S5_GROUP):
                row = lag[pl.ds(q * S5_GROUP + p, 1), :]
                skew = pltpu.roll(jnp.broadcast_to(row, (CHUNK, 2 * CHUNK)), 0, axis=1, stride=1, stride_axis=0)
                toep[pl.ds(pl.multiple_of(q * CHUNK, CHUNK), CHUNK), p * LANES:(p + 1) * LANES] = (
                    skew[:, CHUNK:].astype(BF16))
            return carry

        lax.fori_loop(0, S5_GROUP, toeplitz_rows, 0)

        def outer(dst, f1, f2, powers):
            swapped = jnp.concatenate([pltpu.roll(powers[:, :STATE_LANES], S5_STATE, axis=1),
                                       pltpu.roll(powers[:, STATE_LANES:], S5_STATE, axis=1)], axis=1)
            for ch in range(S5_GROUP):
                blk = gen_ref[0, f1 + ch:f1 + ch + 1, :] * powers + gen_ref[0, f2 + ch:f2 + ch + 1, :] * swapped
                dst[ch * CHUNK:(ch + 1) * CHUNK, :] = blk.astype(BF16)

        outer(ws, 0, S5_GROUP, gen_ref[0, 4 * S5_GROUP:4 * S5_GROUP + CHUNK, :])
        outer(wct, 2 * S5_GROUP, 3 * S5_GROUP, gen_ref[0, 4 * S5_GROUP + CHUNK:, :])

        x = xcat[...]
        y = _dot(x, toep[...])
        summ = _dot(x, ws[...])

        idx = lax.broadcasted_iota(jnp.int32, (nc, LANES), 0)
        pos, left, start = idx, idx, 0
        for n, per_seq in seqs:
            inside = (idx >= start) & (idx < start + n)
            p_ = (idx - start) & (per_seq - 1)
            pos = jnp.where(inside, p_, pos)
            left = jnp.where(inside, per_seq - 1 - p_, left)
            start += n
        max_seq = max(per_seq for _, per_seq in seqs)

        def scan(h, col, behind, shift_sign):
            k, i = 1, 0
            while k < max_seq:
                nb = pltpu.roll(h, (shift_sign * k) % nc, axis=0)
                swapped = pltpu.roll(nb, S5_STATE, axis=1)
                a = step_ref[0, i:i + 1, col:col + STATE_LANES]
                b = step_ref[0, 6 + i:7 + i, col:col + STATE_LANES]
                h = h + jnp.where(behind >= k, a * nb + b * swapped, 0.0)
                k, i = 2 * k, i + 1
            carried = pltpu.roll(h, shift_sign % nc, axis=0)
            return jnp.where(behind >= 1, carried, 0.0)

        h_prev = scan(summ[:, :STATE_LANES], 0, pos, 1)
        h_next = scan(summ[:, STATE_LANES:], STATE_LANES, left, -1)
        carried = jnp.concatenate([h_prev, h_next], axis=1).astype(BF16)
        y = y + lax.dot_general(carried, wct[...], (((1,), (1,)), ((), ())), preferred_element_type=F32)

        for u_ref, y_ref, start, (n, _) in zip(up_refs, yp_refs, starts, seqs):
            for p in range(S5_GROUP):
                y_ref[p] = y[start:start + n, p * LANES:(p + 1) * LANES] + d_ref[0, p:p + 1, :] * u_ref[p]

    return kernel, nc


def _s5_chunk(ups, seq_lens, tables):
    seqs = []
    for up, l in zip(ups, seq_lens):
        per_seq = l // CHUNK
        assert l % CHUNK == 0 and per_seq & (per_seq - 1) == 0 and up.shape[1] % per_seq == 0
        seqs.append((up.shape[1], per_seq))
    kernel, nc = _make_s5_kernel(tuple(seqs))
    per_group = lambda a: pl.BlockSpec((1,) + a.shape[1:], lambda g: (g, 0, 0))
    io = [pl.BlockSpec((S5_GROUP,) + up.shape[1:], lambda g: (g, 0, 0)) for up in ups]
    return pl.pallas_call(
        kernel,
        grid=(S5_GROUPS,),
        in_specs=io + [per_group(t) for t in tables],
        out_specs=io,
        out_shape=[jax.ShapeDtypeStruct(up.shape, F32) for up in ups],
        scratch_shapes=[pltpu.VMEM((nc, GK), BF16), pltpu.VMEM((GK, GK), BF16),
                        pltpu.VMEM((S5_GROUP * S5_GROUP, 2 * CHUNK), F32),
                        pltpu.VMEM((GK, 2 * STATE_LANES), BF16), pltpu.VMEM((GK, 2 * STATE_LANES), BF16)],
        compiler_params=pltpu.CompilerParams(dimension_semantics=("parallel",), vmem_limit_bytes=VMEM_LIMIT),
        name="s5_chunk",
    )(*ups, *tables)


def _mix_attn_kernel(x_ref, yp_ref, z_ref, zprev_ref, znext_ref, gb_ref, gates_ref,
                     kt_ref, v_ref, wglu_ref, cw_ref, wco_ref, wo_ref, gx_ref, wq_ref,
                     wxo_ref, h_ref):
    i = pl.program_id(1)
    n_i = pl.num_programs(1)

    planes = _planes_view(yp_ref)
    yt = jnp.concatenate(
        [jnp.concatenate([planes[_plane_rows(r8, c), :] for c in range(TILE_CHUNKS)], axis=1)
         for r8 in range(D_S5 // SUBLANES)], axis=0)
    yt = _gelu_tanh(yt).astype(BF16)
    glu = lax.dot_general(yt, wglu_ref[...], (((0,), (0,)), ((), ())), preferred_element_type=F32)
    s5_out = glu[:, :D_MODEL] * _sigmoid(glu[:, D_MODEL:])

    z = z_ref[0].astype(F32)
    z_prev = jnp.where(i > 0, zprev_ref[0, 15:16, :].astype(F32), 0.0)
    z_next = jnp.where(i < n_i - 1, znext_ref[0, 0:1, :].astype(F32), 0.0)
    zc = (cw_ref[0:1, :] * pltpu.roll(z, 1, axis=0) + cw_ref[1:2, :] * z
          + cw_ref[2:3, :] * pltpu.roll(z, TM - 1, axis=0))
    edge = lax.broadcasted_iota(jnp.int32, (SUBLANES, D_CONV), 0)
    fix_top = jnp.where(edge == 0, cw_ref[0:1, :] * (z_prev - z[TM - 1:TM, :]), 0.0)
    fix_bot = jnp.where(edge == SUBLANES - 1, cw_ref[2:3, :] * (z_next - z[0:1, :]), 0.0)
    zc = jnp.concatenate([zc[:SUBLANES] + fix_top, zc[SUBLANES:TM - SUBLANES], zc[TM - SUBLANES:] + fix_bot], axis=0)
    conv_out = _dot((gb_ref[0].astype(F32) * zc).astype(BF16), wco_ref[...])

    merged = (gates_ref[0, :, :D_MODEL].astype(F32) * s5_out
              + gates_ref[0, :, D_MODEL:].astype(F32) * conv_out)
    h = x_ref[0] + _dot(merged.astype(BF16), wo_ref[...])

    hn = _rms(h, gx_ref[...]).astype(BF16)
    q = _dot(hn, wq_ref[...]).astype(BF16)
    heads = []
    for hd in range(XATTN_HEADS):
        hs = slice(hd * XATTN_HEAD_DIM, (hd + 1) * XATTN_HEAD_DIM)
        s = _dot(q[:, hs], kt_ref[0, hs, :]) * (XATTN_HEAD_DIM ** -0.5)
        e = jnp.exp(s - jnp.max(s, axis=-1, keepdims=True))
        p = (e / jnp.sum(e, axis=-1, keepdims=True)).astype(BF16)
        heads.append(_dot(p, v_ref[0, :, hs]).astype(BF16))
    o = jnp.concatenate(heads, axis=-1)
    h_ref[0] = h + _dot(o, wxo_ref[...])


def _mix_attn(x, yp, z, gb, gates, kt, v, w_glu, conv_w, w_conv_out, w_o, norm_xattn_g, w_q, w_xo):
    b, l, _ = x.shape
    n_i = l // TM
    tile = lambda w: pl.BlockSpec((1, TM, w), lambda bi, i: (bi, i, 0))
    planes = pl.BlockSpec((D_S5, TILE_CHUNKS, LANES), lambda bi, i: (0, bi * n_i + i, 0))
    halo = TM // 16
    z_prev = pl.BlockSpec((1, 16, D_CONV), lambda bi, i: (bi, jnp.maximum(i * halo - 1, 0), 0))
    z_next = pl.BlockSpec((1, 16, D_CONV), lambda bi, i: (bi, jnp.minimum((i + 1) * halo, l // 16 - 1), 0))
    per_batch = lambda r, c: pl.BlockSpec((1, r, c), lambda bi, i: (bi, 0, 0))
    return pl.pallas_call(
        _mix_attn_kernel,
        grid=(b, n_i),
        in_specs=[tile(D_MODEL), planes, tile(D_CONV), z_prev, z_next,
                  tile(D_CONV), tile(2 * D_MODEL), per_batch(D_MODEL, N_MEM), per_batch(N_MEM, D_MODEL),
                  _const_spec((D_S5, 2 * D_MODEL)),
                  _const_spec((3, D_CONV)), _const_spec((D_CONV, D_MODEL)), _const_spec((D_MODEL, D_MODEL)),
                  _const_spec((1, D_MODEL)), _const_spec((D_MODEL, D_MODEL)), _const_spec((D_MODEL, D_MODEL))],
        out_specs=tile(D_MODEL),
        out_shape=jax.ShapeDtypeStruct(x.shape, F32),
        compiler_params=pltpu.CompilerParams(dimension_semantics=("parallel", "parallel"),
                                             vmem_limit_bytes=VMEM_LIMIT),
        name="mix_attn",
    )(x, yp, z, z, z, gb, gates, kt, v, w_glu, conv_w, w_conv_out, w_o, norm_xattn_g, w_q, w_xo)


def _ffn_kernel(h_ref, gf_ref, wgu_ref, wd_ref, gout_ref, o_ref):
    h = h_ref[...]
    hn = _rms(h, gf_ref[...]).astype(BF16)
    acc = h
    for lo in range(0, D_FF, FF_CHUNK):
        hi = min(lo + FF_CHUNK, D_FF)
        gate = _dot(hn, wgu_ref[:, lo:hi])
        up = _dot(hn, wgu_ref[:, D_FF + lo:D_FF + hi])
        act = (gate * _sigmoid(gate) * up).astype(BF16)
        acc = acc + _dot(act, wd_ref[lo:hi, :])
    o_ref[...] = _rms(acc, gout_ref[...])


def _ffn_out(h2d, norm_ffn_g, w_gate_up, w_down, norm_final_g):
    n = h2d.shape[0]
    row = pl.BlockSpec((TM, D_MODEL), lambda i: (i, 0))
    return pl.pallas_call(
        _ffn_kernel,
        grid=(n // TM,),
        in_specs=[row, _const_spec((1, D_MODEL)), _const_spec((D_MODEL, 2 * D_FF)),
                  _const_spec((D_FF, D_MODEL)), _const_spec((1, D_MODEL))],
        out_specs=row,
        out_shape=jax.ShapeDtypeStruct(h2d.shape, F32),
        compiler_params=pltpu.CompilerParams(dimension_semantics=("parallel",), vmem_limit_bytes=VMEM_LIMIT),
        name="ffn_out",
    )(h2d, norm_ffn_g, w_gate_up, w_down, norm_final_g)


def _front(x, mem, w):
    b, l, _ = x.shape
    up, z, gb, gates = _in_proj(x.reshape(b * l, D_MODEL), w["norm_mix_g"], w["w_ut"], w["w_in_rest"], w["b_gate"])
    k, v = _mem_kv(mem, w["norm_mem_g"], w["w_kv"])
    shp = lambda a: a.reshape(b, l, a.shape[-1])
    return up, shp(z), shp(gb), shp(gates), jnp.swapaxes(k, 1, 2), v


def _back(x, yp, front, w):
    _, z, gb, gates, kt, v = front
    b, l, _ = x.shape
    h = _mix_attn(x, yp, z, gb, gates, kt, v, w["w_glu"], w["conv_w"], w["w_conv_out"], w["w_o"],
                  w["norm_xattn_g"], w["w_q"], w["w_xo"])
    out = _ffn_out(h.reshape(b * l, D_MODEL), w["norm_ffn_g"], w["w_gate_up"], w["w_down"], w["norm_final_g"])
    return out.reshape(b, l, D_MODEL)


def kernel(x_prompt, x_sample, mem_prompt, mem_sample, norm_mix_g, w_in, b_gate, s5_a_re, s5_a_im, s5_log_dt, s5_b_re, s5_b_im, s5_c_re, s5_c_im, s5_d, w_glu_a, w_glu_b, conv_w, w_conv_out, w_o, norm_xattn_g, norm_mem_g, w_q, w_kv, w_xo, norm_ffn_g, w_gate_up, w_down, norm_final_g):
    row = lambda a: a.reshape(1, -1).astype(F32)
    w = {
        "norm_mix_g": row(norm_mix_g[0]), "w_ut": w_in[0, :, :D_S5].T.astype(BF16),
        "w_in_rest": w_in[0, :, D_S5:].astype(BF16), "b_gate": row(b_gate[0]),
        "w_glu": jnp.concatenate([w_glu_a[0], w_glu_b[0]], axis=1).astype(BF16),
        "conv_w": conv_w[0].astype(F32), "w_conv_out": w_conv_out[0].astype(BF16), "w_o": w_o[0].astype(BF16),
        "norm_xattn_g": row(norm_xattn_g[0]), "norm_mem_g": row(norm_mem_g[0]),
        "w_q": w_q[0].astype(BF16), "w_kv": w_kv[0].astype(BF16), "w_xo": w_xo[0].astype(BF16),
        "norm_ffn_g": row(norm_ffn_g[0]), "w_gate_up": w_gate_up[0].astype(BF16),
        "w_down": w_down[0].astype(BF16), "norm_final_g": row(norm_final_g),
    }
    tables = _s5_tables(s5_a_re[0], s5_a_im[0], s5_log_dt[0], s5_b_re[0], s5_b_im[0], s5_c_re[0], s5_c_im[0])
    skip = jnp.broadcast_to(s5_d[0].astype(F32).reshape(S5_GROUPS, S5_GROUP, 1), (S5_GROUPS, S5_GROUP, LANES))
    fp, fs = _front(x_prompt, mem_prompt, w), _front(x_sample, mem_sample, w)
    ypp, yps = _s5_chunk([fp[0], fs[0]], [x_prompt.shape[1], x_sample.shape[1]], tables + (skip,))
    return _back(x_prompt, ypp, fp, w), _back(x_sample, yps, fs, w)
```

```python
import math

import jax
import jax.numpy as jnp
from jax import lax
from jax.experimental import pallas as pl
from jax.experimental.pallas import tpu as pltpu

F32 = jnp.float32
BF16 = jnp.bfloat16

D_MODEL = 1024
D_S5 = D_MODEL // 2
S5_GROUP = 16
S5_GROUPS = D_S5 // S5_GROUP
S5_STATE = 64
D_CONV = D_MODEL // 2
N_MEM = 256
XATTN_HEADS = 4
XATTN_HEAD_DIM = D_MODEL // XATTN_HEADS
D_FF = -(-8 * D_MODEL // (3 * 256)) * 256
EPS = 1e-6

LANES = 128
SUBLANES = 8
VMEM_LIMIT = 56 * 1024 * 1024

MXU_DIM = 256
TM = 1024
CHUNK = LANES
TILE_CHUNKS = TM // CHUNK
STATE_LANES = 2 * S5_STATE
GK = S5_GROUP * CHUNK
FF_CHUNK = 3 * MXU_DIM


def _const_spec(shape):
    zeros = (0,) * len(shape)
    return pl.BlockSpec(shape, lambda *_: zeros, pipeline_mode=pl.Buffered(1))


def _rms(x, g):
    ms = jnp.mean(x * x, axis=-1, keepdims=True)
    return x * lax.rsqrt(ms + EPS) * g


def _sigmoid(x):
    return 1.0 / (1.0 + jnp.exp(-x))


def _gelu_tanh(x):
    c = math.sqrt(2.0 / math.pi)
    return x * (0.5 + 0.5 * jnp.tanh(x * (c + (c * 0.044715) * (x * x))))


def _dot(a, b):
    return jnp.dot(a, b, preferred_element_type=F32)


def _planes_view(ref):
    assert ref.shape[1] == TILE_CHUNKS and ref.shape[2] == LANES
    return ref.reshape(ref.shape[0] * TILE_CHUNKS, LANES)


def _plane_rows(r8, c):
    return pl.ds(r8 * SUBLANES * TILE_CHUNKS + c, SUBLANES, stride=TILE_CHUNKS)


def _in_proj_kernel(x_ref, g_ref, wut_ref, w_ref, bg_ref, up_ref, z_ref, gb_ref, gates_ref):
    xn = _rms(x_ref[...], g_ref[...]).astype(BF16)
    g = _dot(xn, w_ref[:, 3 * D_CONV:])
    gates_ref[...] = _sigmoid(g + bg_ref[...]).astype(BF16)
    ut = lax.dot_general(wut_ref[...], xn, (((1,), (1,)), ((), ())), preferred_element_type=F32)
    planes = _planes_view(up_ref)
    for r8 in range(D_S5 // SUBLANES):
        for c in range(TILE_CHUNKS):
            planes[_plane_rows(r8, c), :] = ut[r8 * SUBLANES:(r8 + 1) * SUBLANES, c * LANES:(c + 1) * LANES]
    x_c = _dot(xn, w_ref[:, 0:D_CONV])
    gate_c = _dot(xn, w_ref[:, 2 * D_CONV:3 * D_CONV])
    z_ref[...] = (gate_c * x_c).astype(BF16)
    gb_ref[...] = _dot(xn, w_ref[:, D_CONV:2 * D_CONV]).astype(BF16)


def _in_proj(x2d, norm_g, w_ut, w_rest, b_gate):
    n = x2d.shape[0]
    row = lambda w: pl.BlockSpec((TM, w), lambda i: (i, 0))
    return pl.pallas_call(
        _in_proj_kernel,
        grid=(n // TM,),
        in_specs=[row(D_MODEL), _const_spec((1, D_MODEL)), _const_spec(w_ut.shape), _const_spec(w_rest.shape),
                  _const_spec((1, 2 * D_MODEL))],
        out_specs=[pl.BlockSpec((D_S5, TILE_CHUNKS, LANES), lambda i: (0, i, 0)), row(D_CONV), row(D_CONV),
                   row(2 * D_MODEL)],
        out_shape=[jax.ShapeDtypeStruct((D_S5, n // CHUNK, LANES), F32), jax.ShapeDtypeStruct((n, D_CONV), BF16),
                   jax.ShapeDtypeStruct((n, D_CONV), BF16), jax.ShapeDtypeStruct((n, 2 * D_MODEL), BF16)],
        compiler_params=pltpu.CompilerParams(dimension_semantics=("parallel",), vmem_limit_bytes=VMEM_LIMIT),
        name="in_proj",
    )(x2d, norm_g, w_ut, w_rest, b_gate)


def _mem_kv_kernel(m_ref, g_ref, w_ref, k_ref, v_ref):
    mn = _rms(m_ref[0], g_ref[...]).astype(BF16)
    k_ref[0] = _dot(mn, w_ref[:, :D_MODEL]).astype(BF16)
    v_ref[0] = _dot(mn, w_ref[:, D_MODEL:]).astype(BF16)


def _mem_kv(mem, norm_g, w_kv):
    b = mem.shape[0]
    blk = pl.BlockSpec((1, N_MEM, D_MODEL), lambda i: (i, 0, 0))
    return pl.pallas_call(
        _mem_kv_kernel,
        grid=(b,),
        in_specs=[blk, _const_spec((1, D_MODEL)), _const_spec((D_MODEL, 2 * D_MODEL))],
        out_specs=[blk, blk],
        out_shape=[jax.ShapeDtypeStruct((b, N_MEM, D_MODEL), BF16)] * 2,
        compiler_params=pltpu.CompilerParams(dimension_semantics=("parallel",), vmem_limit_bytes=VMEM_LIMIT),
        name="mem_kv",
    )(mem, norm_g, w_kv)


def _s5_tables(a_re, a_im, log_dt, b_re, b_im, c_re, c_im):
    f = lambda v: v.astype(F32)
    a_re, a_im, b_re, b_im, c_re, c_im = map(f, (a_re, a_im, b_re, b_im, c_re, c_im))
    dt = jnp.exp(f(log_dt))[..., None]
    lr, li = a_re * dt, a_im * dt

    def apow(k):
        m = jnp.exp(lr[..., None] * k)
        return m * jnp.cos(li[..., None] * k), m * jnp.sin(li[..., None] * k)

    ab_re, ab_im = jnp.exp(lr) * jnp.cos(li), jnp.exp(lr) * jnp.sin(li)
    x, y, den = ab_re - 1.0, ab_im, a_re * a_re + a_im * a_im
    cf_re, cf_im = (x * a_re + y * a_im) / den, (y * a_re - x * a_im) / den
    bb_re = cf_re[..., None] * b_re - cf_im[..., None] * b_im
    bb_im = cf_re[..., None] * b_im + cf_im[..., None] * b_re
    bq_re, bq_im = jnp.swapaxes(bb_re, -1, -2), jnp.swapaxes(bb_im, -1, -2)
    g = a_re.shape[1]

    cbr = c_re[:, :, None] * bq_re[:, :, :, None] - c_im[:, :, None] * bq_im[:, :, :, None]
    cbi = c_re[:, :, None] * bq_im[:, :, :, None] + c_im[:, :, None] * bq_re[:, :, :, None]
    half = jnp.concatenate([cbr, -cbi], axis=-1).reshape(2, g, S5_GROUP * S5_GROUP, STATE_LANES)
    cb = jnp.concatenate([half[0], half[1]], axis=-1)
    j = jnp.arange(2 * CHUNK, dtype=F32)
    pf_re, pf_im = apow(jnp.maximum(j - CHUNK, 0.0))
    pb_re, pb_im = apow(jnp.maximum(CHUNK - j, 0.0))
    mf, mb = (j >= CHUNK).astype(F32), (j <= CHUNK).astype(F32)
    pw = jnp.concatenate([pf_re[0] * mf, pf_im[0] * mf, pb_re[1] * mb, pb_im[1] * mb], axis=1)

    s = jnp.arange(CHUNK, dtype=F32)
    in_re, in_im = apow(jnp.stack([CHUNK - 1.0 - s, s])[:, None, None, :])
    out_re, out_im = apow(jnp.stack([s + 1.0, CHUNK - s])[:, None, None, :])
    both = lambda re, im: jnp.concatenate([re[0], im[0], re[1], im[1]], axis=-1)
    tr = lambda v: jnp.swapaxes(v, -1, -2)
    gen = jnp.concatenate([both(bq_re, bq_re), both(-bq_im, bq_im), both(c_re, -c_re), both(-c_im, -c_im),
                           both(tr(in_re), tr(in_im)), both(tr(out_re), tr(out_im))], axis=1)

    ks = CHUNK * (2.0 ** jnp.arange(6, dtype=F32))
    t_re, t_im = apow(ks)
    t_re, t_im = jnp.moveaxis(t_re, -1, 2), jnp.moveaxis(t_im, -1, 2)
    row_a = jnp.concatenate([t_re[0], t_re[0], t_re[1], t_re[1]], axis=-1)
    row_b = jnp.concatenate([-t_im[0], t_im[0], -t_im[1], t_im[1]], axis=-1)
    step = jnp.concatenate([row_a, row_b], axis=1)
    return cb, pw, gen, step


def _make_s5_kernel(seqs):
    nc = sum(n for n, _ in seqs)
    starts = [sum(n for n, _ in seqs[:k]) for k in range(len(seqs))]

    def kernel(*refs):
        n_in = len(seqs)
        up_refs = refs[:n_in]
        cb_ref, pw_ref, cbn_ref, pwn_ref, gen_ref, step_ref, d_ref = refs[n_in:n_in + 7]
        yp_refs = refs[n_in + 7:2 * n_in + 7]
        xcat, toep, lag, ws, wct = refs[2 * n_in + 7:]
        g = pl.program_id(0)
        slot = g % 2

        for ref, start, (n, _) in zip(up_refs, starts, seqs):
            for q in range(S5_GROUP):
                xcat[start:start + n, q * LANES:(q + 1) * LANES] = ref[q].astype(BF16)

        def toeplitz(cb, pw, dst):
            lag[...] = jnp.dot(cb[0], pw[0], preferred_element_type=F32, precision=lax.Precision.HIGHEST)
            for q in range(S5_GROUP):
                for p in range(S5_GROUP):
                    row = lag[q * S5_GROUP + p:q * S5_GROUP + p + 1, :]
                    skew = pltpu.roll(jnp.broadcast_to(row, (CHUNK, 2 * CHUNK)), 0, axis=1, stride=1, stride_axis=0)
                    dst[q * CHUNK:(q + 1) * CHUNK, p * LANES:(p + 1) * LANES] = skew[:, CHUNK:].astype(BF16)

        @pl.when(g == 0)
        def _():
            toeplitz(cb_ref, pw_ref, toep.at[0])

        toeplitz(cbn_ref, pwn_ref, toep.at[1 - slot])

        def outer(dst, f1, f2, powers):
            swapped = jnp.concatenate([pltpu.roll(powers[:, :STATE_LANES], S5_STATE, axis=1),
                                       pltpu.roll(powers[:, STATE_LANES:], S5_STATE, axis=1)], axis=1)
            for ch in range(S5_GROUP):
                blk = gen_ref[0, f1 + ch:f1 + ch + 1, :] * powers + gen_ref[0, f2 + ch:f2 + ch + 1, :] * swapped
                dst[ch * CHUNK:(ch + 1) * CHUNK, :] = blk.astype(BF16)

        outer(ws, 0, S5_GROUP, gen_ref[0, 4 * S5_GROUP:4 * S5_GROUP + CHUNK, :])
        outer(wct, 2 * S5_GROUP, 3 * S5_GROUP, gen_ref[0, 4 * S5_GROUP + CHUNK:, :])

        x = xcat[...]
        y = _dot(x, toep[slot])
        summ = _dot(x, ws[...])

        idx = lax.broadcasted_iota(jnp.int32, (nc, LANES), 0)
        pos, left, start = idx, idx, 0
        for n, per_seq in seqs:
            inside = (idx >= start) & (idx < start + n)
            p_ = (idx - start) & (per_seq - 1)
            pos = jnp.where(inside, p_, pos)
            left = jnp.where(inside, per_seq - 1 - p_, left)
            start += n
        max_seq = max(per_seq for _, per_seq in seqs)

        def scan(h, col, behind, shift_sign):
            k, i = 1, 0
            while k < max_seq:
                nb = pltpu.roll(h, (shift_sign * k) % nc, axis=0)
                swapped = pltpu.roll(nb, S5_STATE, axis=1)
                a = step_ref[0, i:i + 1, col:col + STATE_LANES]
                b = step_ref[0, 6 + i:7 + i, col:col + STATE_LANES]
                h = h + jnp.where(behind >= k, a * nb + b * swapped, 0.0)
                k, i = 2 * k, i + 1
            carried = pltpu.roll(h, shift_sign % nc, axis=0)
            return jnp.where(behind >= 1, carried, 0.0)

        h_prev = scan(summ[:, :STATE_LANES], 0, pos, 1)
        h_next = scan(summ[:, STATE_LANES:], STATE_LANES, left, -1)
        carried = jnp.concatenate([h_prev, h_next], axis=1).astype(BF16)
        y = y + lax.dot_general(carried, wct[...], (((1,), (1,)), ((), ())), preferred_element_type=F32)

        for u_ref, y_ref, start, (n, _) in zip(up_refs, yp_refs, starts, seqs):
            for p in range(S5_GROUP):
                y_ref[p] = y[start:start + n, p * LANES:(p + 1) * LANES] + d_ref[0, p:p + 1, :] * u_ref[p]

    return kernel, nc


def _s5_chunk(ups, seq_lens, tables):
    seqs = []
    for up, l in zip(ups, seq_lens):
        per_seq = l // CHUNK
        assert l % CHUNK == 0 and per_seq & (per_seq - 1) == 0 and up.shape[1] % per_seq == 0
        seqs.append((up.shape[1], per_seq))
    kernel, nc = _make_s5_kernel(tuple(seqs))
    per_group = lambda a: pl.BlockSpec((1,) + a.shape[1:], lambda g: (g, 0, 0))
    next_group = lambda a: pl.BlockSpec((1,) + a.shape[1:], lambda g: (jnp.minimum(g + 1, S5_GROUPS - 1), 0, 0))
    io = [pl.BlockSpec((S5_GROUP,) + up.shape[1:], lambda g: (g, 0, 0)) for up in ups]
    cb, pw = tables[:2]
    return pl.pallas_call(
        kernel,
        grid=(S5_GROUPS,),
        in_specs=io + [per_group(cb), per_group(pw), next_group(cb), next_group(pw)] + [per_group(t) for t in tables[2:]],
        out_specs=io,
        out_shape=[jax.ShapeDtypeStruct(up.shape, F32) for up in ups],
        scratch_shapes=[pltpu.VMEM((nc, GK), BF16), pltpu.VMEM((2, GK, GK), BF16),
                        pltpu.VMEM((S5_GROUP * S5_GROUP, 2 * CHUNK), F32),
                        pltpu.VMEM((GK, 2 * STATE_LANES), BF16), pltpu.VMEM((GK, 2 * STATE_LANES), BF16)],
        compiler_params=pltpu.CompilerParams(dimension_semantics=("arbitrary",), vmem_limit_bytes=VMEM_LIMIT),
        name="s5_chunk",
    )(*ups, cb, pw, cb, pw, *tables[2:])


def _mix_attn_kernel(x_ref, yp_ref, z_ref, zprev_ref, znext_ref, gb_ref, gates_ref,
                     kt_ref, v_ref, wglu_ref, cw_ref, wco_ref, wo_ref, gx_ref, wq_ref,
                     wxo_ref, h_ref):
    i = pl.program_id(1)
    n_i = pl.num_programs(1)

    planes = _planes_view(yp_ref)
    yt = jnp.concatenate(
        [jnp.concatenate([planes[_plane_rows(r8, c), :] for c in range(TILE_CHUNKS)], axis=1)
         for r8 in range(D_S5 // SUBLANES)], axis=0)
    yt = _gelu_tanh(yt).astype(BF16)
    glu = lax.dot_general(yt, wglu_ref[...], (((0,), (0,)), ((), ())), preferred_element_type=F32)
    s5_out = glu[:, :D_MODEL] * _sigmoid(glu[:, D_MODEL:])

    z = z_ref[0].astype(F32)
    z_prev = jnp.where(i > 0, zprev_ref[0, 15:16, :].astype(F32), 0.0)
    z_next = jnp.where(i < n_i - 1, znext_ref[0, 0:1, :].astype(F32), 0.0)
    zc = (cw_ref[0:1, :] * pltpu.roll(z, 1, axis=0) + cw_ref[1:2, :] * z
          + cw_ref[2:3, :] * pltpu.roll(z, TM - 1, axis=0))
    edge = lax.broadcasted_iota(jnp.int32, (SUBLANES, D_CONV), 0)
    fix_top = jnp.where(edge == 0, cw_ref[0:1, :] * (z_prev - z[TM - 1:TM, :]), 0.0)
    fix_bot = jnp.where(edge == SUBLANES - 1, cw_ref[2:3, :] * (z_next - z[0:1, :]), 0.0)
    zc = jnp.concatenate([zc[:SUBLANES] + fix_top, zc[SUBLANES:TM - SUBLANES], zc[TM - SUBLANES:] + fix_bot], axis=0)
    conv_out = _dot((gb_ref[0].astype(F32) * zc).astype(BF16), wco_ref[...])

    merged = (gates_ref[0, :, :D_MODEL].astype(F32) * s5_out
              + gates_ref[0, :, D_MODEL:].astype(F32) * conv_out)
    h = x_ref[0] + _dot(merged.astype(BF16), wo_ref[...])

    hn = _rms(h, gx_ref[...]).astype(BF16)
    q = _dot(hn, wq_ref[...]).astype(BF16)
    heads = []
    for hd in range(XATTN_HEADS):
        hs = slice(hd * XATTN_HEAD_DIM, (hd + 1) * XATTN_HEAD_DIM)
        s = _dot(q[:, hs], kt_ref[0, hs, :]) * (XATTN_HEAD_DIM ** -0.5)
        e = jnp.exp(s - jnp.max(s, axis=-1, keepdims=True))
        p = (e / jnp.sum(e, axis=-1, keepdims=True)).astype(BF16)
        heads.append(_dot(p, v_ref[0, :, hs]).astype(BF16))
    o = jnp.concatenate(heads, axis=-1)
    h_ref[0] = h + _dot(o, wxo_ref[...])


def _mix_attn(x, yp, z, gb, gates, kt, v, w_glu, conv_w, w_conv_out, w_o, norm_xattn_g, w_q, w_xo):
    b, l, _ = x.shape
    n_i = l // TM
    tile = lambda w: pl.BlockSpec((1, TM, w), lambda bi, i: (bi, i, 0))
    planes = pl.BlockSpec((D_S5, TILE_CHUNKS, LANES), lambda bi, i: (0, bi * n_i + i, 0))
    halo = TM // 16
    z_prev = pl.BlockSpec((1, 16, D_CONV), lambda bi, i: (bi, jnp.maximum(i * halo - 1, 0), 0))
    z_next = pl.BlockSpec((1, 16, D_CONV), lambda bi, i: (bi, jnp.minimum((i + 1) * halo, l // 16 - 1), 0))
    per_batch = lambda r, c: pl.BlockSpec((1, r, c), lambda bi, i: (bi, 0, 0))
    return pl.pallas_call(
        _mix_attn_kernel,
        grid=(b, n_i),
        in_specs=[tile(D_MODEL), planes, tile(D_CONV), z_prev, z_next,
                  tile(D_CONV), tile(2 * D_MODEL), per_batch(D_MODEL, N_MEM), per_batch(N_MEM, D_MODEL),
                  _const_spec((D_S5, 2 * D_MODEL)),
                  _const_spec((3, D_CONV)), _const_spec((D_CONV, D_MODEL)), _const_spec((D_MODEL, D_MODEL)),
                  _const_spec((1, D_MODEL)), _const_spec((D_MODEL, D_MODEL)), _const_spec((D_MODEL, D_MODEL))],
        out_specs=tile(D_MODEL),
        out_shape=jax.ShapeDtypeStruct(x.shape, F32),
        compiler_params=pltpu.CompilerParams(dimension_semantics=("parallel", "parallel"),
                                             vmem_limit_bytes=VMEM_LIMIT),
        name="mix_attn",
    )(x, yp, z, z, z, gb, gates, kt, v, w_glu, conv_w, w_conv_out, w_o, norm_xattn_g, w_q, w_xo)


def _ffn_kernel(h_ref, gf_ref, wgu_ref, wd_ref, gout_ref, o_ref):
    h = h_ref[...]
    hn = _rms(h, gf_ref[...]).astype(BF16)
    acc = h
    for lo in range(0, D_FF, FF_CHUNK):
        hi = min(lo + FF_CHUNK, D_FF)
        gate = _dot(hn, wgu_ref[:, lo:hi])
        up = _dot(hn, wgu_ref[:, D_FF + lo:D_FF + hi])
        act = (gate * _sigmoid(gate) * up).astype(BF16)
        acc = acc + _dot(act, wd_ref[lo:hi, :])
    o_ref[...] = _rms(acc, gout_ref[...])


def _ffn_out(h2d, norm_ffn_g, w_gate_up, w_down, norm_final_g):
    n = h2d.shape[0]
    row = pl.BlockSpec((TM, D_MODEL), lambda i: (i, 0))
    return pl.pallas_call(
        _ffn_kernel,
        grid=(n // TM,),
        in_specs=[row, _const_spec((1, D_MODEL)), _const_spec((D_MODEL, 2 * D_FF)),
                  _const_spec((D_FF, D_MODEL)), _const_spec((1, D_MODEL))],
        out_specs=row,
        out_shape=jax.ShapeDtypeStruct(h2d.shape, F32),
        compiler_params=pltpu.CompilerParams(dimension_semantics=("parallel",), vmem_limit_bytes=VMEM_LIMIT),
        name="ffn_out",
    )(h2d, norm_ffn_g, w_gate_up, w_down, norm_final_g)


def _front(x, mem, w):
    b, l, _ = x.shape
    up, z, gb, gates = _in_proj(x.reshape(b * l, D_MODEL), w["norm_mix_g"], w["w_ut"], w["w_in_rest"], w["b_gate"])
    k, v = _mem_kv(mem, w["norm_mem_g"], w["w_kv"])
    shp = lambda a: a.reshape(b, l, a.shape[-1])
    return up, shp(z), shp(gb), shp(gates), jnp.swapaxes(k, 1, 2), v


def _back(x, yp, front, w):
    _, z, gb, gates, kt, v = front
    b, l, _ = x.shape
    h = _mix_attn(x, yp, z, gb, gates, kt, v, w["w_glu"], w["conv_w"], w["w_conv_out"], w["w_o"],
                  w["norm_xattn_g"], w["w_q"], w["w_xo"])
    out = _ffn_out(h.reshape(b * l, D_MODEL), w["norm_ffn_g"], w["w_gate_up"], w["w_down"], w["norm_final_g"])
    return out.reshape(b, l, D_MODEL)


def kernel(x_prompt, x_sample, mem_prompt, mem_sample, norm_mix_g, w_in, b_gate, s5_a_re, s5_a_im, s5_log_dt, s5_b_re, s5_b_im, s5_c_re, s5_c_im, s5_d, w_glu_a, w_glu_b, conv_w, w_conv_out, w_o, norm_xattn_g, norm_mem_g, w_q, w_kv, w_xo, norm_ffn_g, w_gate_up, w_down, norm_final_g):
    row = lambda a: a.reshape(1, -1).astype(F32)
    w = {
        "norm_mix_g": row(norm_mix_g[0]), "w_ut": w_in[0, :, :D_S5].T.astype(BF16),
        "w_in_rest": w_in[0, :, D_S5:].astype(BF16), "b_gate": row(b_gate[0]),
        "w_glu": jnp.concatenate([w_glu_a[0], w_glu_b[0]], axis=1).astype(BF16),
        "conv_w": conv_w[0].astype(F32), "w_conv_out": w_conv_out[0].astype(BF16), "w_o": w_o[0].astype(BF16),
        "norm_xattn_g": row(norm_xattn_g[0]), "norm_mem_g": row(norm_mem_g[0]),
        "w_q": w_q[0].astype(BF16), "w_kv": w_kv[0].astype(BF16), "w_xo": w_xo[0].astype(BF16),
        "norm_ffn_g": row(norm_ffn_g[0]), "w_gate_up": w_gate_up[0].astype(BF16),
        "w_down": w_down[0].astype(BF16), "norm_final_g": row(norm_final_g),
    }
    tables = _s5_tables(s5_a_re[0], s5_a_im[0], s5_log_dt[0], s5_b_re[0], s5_b_im[0], s5_c_re[0], s5_c_im[0])
    skip = jnp.broadcast_to(s5_d[0].astype(F32).reshape(S5_GROUPS, S5_GROUP, 1), (S5_GROUPS, S5_GROUP, LANES))
    fp, fs = _front(x_prompt, mem_prompt, w), _front(x_sample, mem_sample, w)
    ypp, yps = _s5_chunk([fp[0], fs[0]], [x_prompt.shape[1], x_sample.shape[1]], tables + (skip,))
    return _back(x_prompt, ypp, fp, w), _back(x_sample, yps, fs, w)
```

```python
import math

import jax
import jax.numpy as jnp
from jax import lax
from jax.experimental import pallas as pl
from jax.experimental.pallas import tpu as pltpu

F32 = jnp.float32
BF16 = jnp.bfloat16

D_MODEL = 1024
D_S5 = D_MODEL // 2
S5_GROUP = 16
S5_GROUPS = D_S5 // S5_GROUP
S5_STATE = 64
D_CONV = D_MODEL // 2
N_MEM = 256
XATTN_HEADS = 4
XATTN_HEAD_DIM = D_MODEL // XATTN_HEADS
D_FF = -(-8 * D_MODEL // (3 * 256)) * 256
EPS = 1e-6

LANES = 128
SUBLANES = 8
VMEM_LIMIT = 56 * 1024 * 1024

MXU_DIM = 256
TM = 1024
CHUNK = LANES
TILE_CHUNKS = TM // CHUNK
STATE_LANES = 2 * S5_STATE
GK = S5_GROUP * CHUNK
FF_CHUNK = 3 * MXU_DIM


def _const_spec(shape):
    zeros = (0,) * len(shape)
    return pl.BlockSpec(shape, lambda *_: zeros, pipeline_mode=pl.Buffered(1))


def _rms(x, g):
    ms = jnp.mean(x * x, axis=-1, keepdims=True)
    return x * lax.rsqrt(ms + EPS) * g


def _sigmoid(x):
    return 1.0 / (1.0 + jnp.exp(-x))


def _gelu_tanh(x):
    c = math.sqrt(2.0 / math.pi)
    return x * (0.5 + 0.5 * jnp.tanh(x * (c + (c * 0.044715) * (x * x))))


def _dot(a, b):
    return jnp.dot(a, b, preferred_element_type=F32)


def _planes_view(ref):
    assert ref.shape[1] == TILE_CHUNKS and ref.shape[2] == LANES
    return ref.reshape(ref.shape[0] * TILE_CHUNKS, LANES)


def _plane_rows(r8, c):
    return pl.ds(r8 * SUBLANES * TILE_CHUNKS + c, SUBLANES, stride=TILE_CHUNKS)


def _in_proj_kernel(x_ref, g_ref, wut_ref, w_ref, bg_ref, up_ref, z_ref, gb_ref, gates_ref):
    xn = _rms(x_ref[...], g_ref[...]).astype(BF16)
    g = _dot(xn, w_ref[:, 3 * D_CONV:])
    gates_ref[...] = _sigmoid(g + bg_ref[...]).astype(BF16)
    ut = lax.dot_general(wut_ref[...], xn, (((1,), (1,)), ((), ())), preferred_element_type=F32)
    planes = _planes_view(up_ref)
    for r8 in range(D_S5 // SUBLANES):
        for c in range(TILE_CHUNKS):
            planes[_plane_rows(r8, c), :] = ut[r8 * SUBLANES:(r8 + 1) * SUBLANES, c * LANES:(c + 1) * LANES]
    x_c = _dot(xn, w_ref[:, 0:D_CONV])
    gate_c = _dot(xn, w_ref[:, 2 * D_CONV:3 * D_CONV])
    z_ref[...] = (gate_c * x_c).astype(BF16)
    gb_ref[...] = _dot(xn, w_ref[:, D_CONV:2 * D_CONV]).astype(BF16)


def _in_proj(x2d, norm_g, w_ut, w_rest, b_gate):
    n = x2d.shape[0]
    row = lambda w: pl.BlockSpec((TM, w), lambda i: (i, 0))
    return pl.pallas_call(
        _in_proj_kernel,
        grid=(n // TM,),
        in_specs=[row(D_MODEL), _const_spec((1, D_MODEL)), _const_spec(w_ut.shape), _const_spec(w_rest.shape),
                  _const_spec((1, 2 * D_MODEL))],
        out_specs=[pl.BlockSpec((D_S5, TILE_CHUNKS, LANES), lambda i: (0, i, 0)), row(D_CONV), row(D_CONV),
                   row(2 * D_MODEL)],
        out_shape=[jax.ShapeDtypeStruct((D_S5, n // CHUNK, LANES), F32), jax.ShapeDtypeStruct((n, D_CONV), BF16),
                   jax.ShapeDtypeStruct((n, D_CONV), BF16), jax.ShapeDtypeStruct((n, 2 * D_MODEL), BF16)],
        compiler_params=pltpu.CompilerParams(dimension_semantics=("parallel",), vmem_limit_bytes=VMEM_LIMIT),
        name="in_proj",
    )(x2d, norm_g, w_ut, w_rest, b_gate)


def _mem_kv_kernel(m_ref, g_ref, w_ref, k_ref, v_ref):
    mn = _rms(m_ref[0], g_ref[...]).astype(BF16)
    k_ref[0] = _dot(mn, w_ref[:, :D_MODEL]).astype(BF16)
    v_ref[0] = _dot(mn, w_ref[:, D_MODEL:]).astype(BF16)


def _mem_kv(mem, norm_g, w_kv):
    b = mem.shape[0]
    blk = pl.BlockSpec((1, N_MEM, D_MODEL), lambda i: (i, 0, 0))
    return pl.pallas_call(
        _mem_kv_kernel,
        grid=(b,),
        in_specs=[blk, _const_spec((1, D_MODEL)), _const_spec((D_MODEL, 2 * D_MODEL))],
        out_specs=[blk, blk],
        out_shape=[jax.ShapeDtypeStruct((b, N_MEM, D_MODEL), BF16)] * 2,
        compiler_params=pltpu.CompilerParams(dimension_semantics=("parallel",), vmem_limit_bytes=VMEM_LIMIT),
        name="mem_kv",
    )(mem, norm_g, w_kv)


def _s5_tables(a_re, a_im, log_dt, b_re, b_im, c_re, c_im):
    f = lambda v: v.astype(F32)
    a_re, a_im, b_re, b_im, c_re, c_im = map(f, (a_re, a_im, b_re, b_im, c_re, c_im))
    dt = jnp.exp(f(log_dt))[..., None]
    lr, li = a_re * dt, a_im * dt

    def apow(k):
        m = jnp.exp(lr[..., None] * k)
        return m * jnp.cos(li[..., None] * k), m * jnp.sin(li[..., None] * k)

    ab_re, ab_im = jnp.exp(lr) * jnp.cos(li), jnp.exp(lr) * jnp.sin(li)
    x, y, den = ab_re - 1.0, ab_im, a_re * a_re + a_im * a_im
    cf_re, cf_im = (x * a_re + y * a_im) / den, (y * a_re - x * a_im) / den
    bb_re = cf_re[..., None] * b_re - cf_im[..., None] * b_im
    bb_im = cf_re[..., None] * b_im + cf_im[..., None] * b_re
    bq_re, bq_im = jnp.swapaxes(bb_re, -1, -2), jnp.swapaxes(bb_im, -1, -2)
    g = a_re.shape[1]

    cbr = c_re[:, :, None] * bq_re[:, :, :, None] - c_im[:, :, None] * bq_im[:, :, :, None]
    cbi = c_re[:, :, None] * bq_im[:, :, :, None] + c_im[:, :, None] * bq_re[:, :, :, None]
    half = jnp.concatenate([cbr, -cbi], axis=-1).reshape(2, g, S5_GROUP * S5_GROUP, STATE_LANES)
    cb = jnp.concatenate([half[0], half[1]], axis=-1)
    j = jnp.arange(2 * CHUNK, dtype=F32)
    pf_re, pf_im = apow(jnp.maximum(j - CHUNK, 0.0))
    pb_re, pb_im = apow(jnp.maximum(CHUNK - j, 0.0))
    mf, mb = (j >= CHUNK).astype(F32), (j <= CHUNK).astype(F32)
    pw = jnp.concatenate([pf_re[0] * mf, pf_im[0] * mf, pb_re[1] * mb, pb_im[1] * mb], axis=1)

    s = jnp.arange(CHUNK, dtype=F32)
    in_re, in_im = apow(jnp.stack([CHUNK - 1.0 - s, s])[:, None, None, :])
    out_re, out_im = apow(jnp.stack([s + 1.0, CHUNK - s])[:, None, None, :])
    both = lambda re, im: jnp.concatenate([re[0], im[0], re[1], im[1]], axis=-1)
    tr = lambda v: jnp.swapaxes(v, -1, -2)
    gen = jnp.concatenate([both(bq_re, bq_re), both(-bq_im, bq_im), both(c_re, -c_re), both(-c_im, -c_im),
                           both(tr(in_re), tr(in_im)), both(tr(out_re), tr(out_im))], axis=1)

    ks = CHUNK * (2.0 ** jnp.arange(6, dtype=F32))
    t_re, t_im = apow(ks)
    t_re, t_im = jnp.moveaxis(t_re, -1, 2), jnp.moveaxis(t_im, -1, 2)
    row_a = jnp.concatenate([t_re[0], t_re[0], t_re[1], t_re[1]], axis=-1)
    row_b = jnp.concatenate([-t_im[0], t_im[0], -t_im[1], t_im[1]], axis=-1)
    step = jnp.concatenate([row_a, row_b], axis=1)
    return cb, pw, gen, step


def _make_s5_kernel(seqs):
    nc = sum(n for n, _ in seqs)
    starts = [sum(n for n, _ in seqs[:k]) for k in range(len(seqs))]

    def kernel(*refs):
        n_in = len(seqs)
        up_refs = refs[:n_in]
        cb_ref, pw_ref, cbn_ref, pwn_ref, gen_ref, step_ref, d_ref = refs[n_in:n_in + 7]
        yp_refs = refs[n_in + 7:2 * n_in + 7]
        xcat, toep, lag, ws, wct = refs[2 * n_in + 7:]
        g = pl.program_id(0)
        slot = g % 2

        for ref, start, (n, _) in zip(up_refs, starts, seqs):
            for q in range(S5_GROUP):
                xcat[start:start + n, q * LANES:(q + 1) * LANES] = ref[q].astype(BF16)

        s_idx = lax.broadcasted_iota(jnp.int32, (CHUNK, CHUNK), 0)
        j_idx = lax.broadcasted_iota(jnp.int32, (CHUNK, CHUNK), 1)
        from_fwd = j_idx + s_idx < CHUNK

        def toeplitz(cb, pw, dst):
            lag[...] = jnp.dot(cb[0], pw[0], preferred_element_type=F32, precision=lax.Precision.HIGHEST)
            for q in range(S5_GROUP):
                for p in range(S5_GROUP):
                    r = q * S5_GROUP + p
                    mixed = jnp.where(from_fwd, lag[r:r + 1, CHUNK:], lag[r:r + 1, :CHUNK])
                    skew = pltpu.roll(mixed, 0, axis=1, stride=1, stride_axis=0)
                    dst[q * CHUNK:(q + 1) * CHUNK, p * LANES:(p + 1) * LANES] = skew.astype(BF16)

        @pl.when(g == 0)
        def _():
            toeplitz(cb_ref, pw_ref, toep.at[0])

        toeplitz(cbn_ref, pwn_ref, toep.at[1 - slot])

        def outer(dst, f1, f2, powers):
            swapped = jnp.concatenate([pltpu.roll(powers[:, :STATE_LANES], S5_STATE, axis=1),
                                       pltpu.roll(powers[:, STATE_LANES:], S5_STATE, axis=1)], axis=1)
            for ch in range(S5_GROUP):
                blk = gen_ref[0, f1 + ch:f1 + ch + 1, :] * powers + gen_ref[0, f2 + ch:f2 + ch + 1, :] * swapped
                dst[ch * CHUNK:(ch + 1) * CHUNK, :] = blk.astype(BF16)

        outer(ws, 0, S5_GROUP, gen_ref[0, 4 * S5_GROUP:4 * S5_GROUP + CHUNK, :])
        outer(wct, 2 * S5_GROUP, 3 * S5_GROUP, gen_ref[0, 4 * S5_GROUP + CHUNK:, :])

        x = xcat[...]
        y = _dot(x, toep[slot])
        summ = _dot(x, ws[...])

        idx = lax.broadcasted_iota(jnp.int32, (nc, LANES), 0)
        pos, left, start = idx, idx, 0
        for n, per_seq in seqs:
            inside = (idx >= start) & (idx < start + n)
            p_ = (idx - start) & (per_seq - 1)
            pos = jnp.where(inside, p_, pos)
            left = jnp.where(inside, per_seq - 1 - p_, left)
            start += n
        max_seq = max(per_seq for _, per_seq in seqs)

        def scan(h, col, behind, shift_sign):
            k, i = 1, 0
            while k < max_seq:
                nb = pltpu.roll(h, (shift_sign * k) % nc, axis=0)
                swapped = pltpu.roll(nb, S5_STATE, axis=1)
                a = step_ref[0, i:i + 1, col:col + STATE_LANES]
                b = step_ref[0, 6 + i:7 + i, col:col + STATE_LANES]
                h = h + jnp.where(behind >= k, a * nb + b * swapped, 0.0)
                k, i = 2 * k, i + 1
            carried = pltpu.roll(h, shift_sign % nc, axis=0)
            return jnp.where(behind >= 1, carried, 0.0)

        h_prev = scan(summ[:, :STATE_LANES], 0, pos, 1)
        h_next = scan(summ[:, STATE_LANES:], STATE_LANES, left, -1)
        carried = jnp.concatenate([h_prev, h_next], axis=1).astype(BF16)
        y = y + lax.dot_general(carried, wct[...], (((1,), (1,)), ((), ())), preferred_element_type=F32)

        for u_ref, y_ref, start, (n, _) in zip(up_refs, yp_refs, starts, seqs):
            for p in range(S5_GROUP):
                y_ref[p] = y[start:start + n, p * LANES:(p + 1) * LANES] + d_ref[0, p:p + 1, :] * u_ref[p]

    return kernel, nc


def _s5_chunk(ups, seq_lens, tables):
    seqs = []
    for up, l in zip(ups, seq_lens):
        per_seq = l // CHUNK
        assert l % CHUNK == 0 and per_seq & (per_seq - 1) == 0 and up.shape[1] % per_seq == 0
        seqs.append((up.shape[1], per_seq))
    kernel, nc = _make_s5_kernel(tuple(seqs))
    per_group = lambda a: pl.BlockSpec((1,) + a.shape[1:], lambda g: (g, 0, 0))
    next_group = lambda a: pl.BlockSpec((1,) + a.shape[1:], lambda g: (jnp.minimum(g + 1, S5_GROUPS - 1), 0, 0))
    io = [pl.BlockSpec((S5_GROUP,) + up.shape[1:], lambda g: (g, 0, 0)) for up in ups]
    cb, pw = tables[:2]
    return pl.pallas_call(
        kernel,
        grid=(S5_GROUPS,),
        in_specs=io + [per_group(cb), per_group(pw), next_group(cb), next_group(pw)] + [per_group(t) for t in tables[2:]],
        out_specs=io,
        out_shape=[jax.ShapeDtypeStruct(up.shape, F32) for up in ups],
        scratch_shapes=[pltpu.VMEM((nc, GK), BF16), pltpu.VMEM((2, GK, GK), BF16),
                        pltpu.VMEM((S5_GROUP * S5_GROUP, 2 * CHUNK), F32),
                        pltpu.VMEM((GK, 2 * STATE_LANES), BF16), pltpu.VMEM((GK, 2 * STATE_LANES), BF16)],
        compiler_params=pltpu.CompilerParams(dimension_semantics=("arbitrary",), vmem_limit_bytes=VMEM_LIMIT),
        name="s5_chunk",
    )(*ups, cb, pw, cb, pw, *tables[2:])


def _mix_attn_kernel(x_ref, yp_ref, z_ref, zprev_ref, znext_ref, gb_ref, gates_ref,
                     kt_ref, v_ref, wglu_ref, cw_ref, wco_ref, wo_ref, gx_ref, wq_ref,
                     wxo_ref, h_ref):
    i = pl.program_id(1)
    n_i = pl.num_programs(1)

    planes = _planes_view(yp_ref)
    yt = jnp.concatenate(
        [jnp.concatenate([planes[_plane_rows(r8, c), :] for c in range(TILE_CHUNKS)], axis=1)
         for r8 in range(D_S5 // SUBLANES)], axis=0)
    yt = _gelu_tanh(yt).astype(BF16)
    glu = lax.dot_general(yt, wglu_ref[...], (((0,), (0,)), ((), ())), preferred_element_type=F32)
    s5_out = glu[:, :D_MODEL] * _sigmoid(glu[:, D_MODEL:])

    z = z_ref[0].astype(F32)
    z_prev = jnp.where(i > 0, zprev_ref[0, 15:16, :].astype(F32), 0.0)
    z_next = jnp.where(i < n_i - 1, znext_ref[0, 0:1, :].astype(F32), 0.0)
    zc = (cw_ref[0:1, :] * pltpu.roll(z, 1, axis=0) + cw_ref[1:2, :] * z
          + cw_ref[2:3, :] * pltpu.roll(z, TM - 1, axis=0))
    edge = lax.broadcasted_iota(jnp.int32, (SUBLANES, D_CONV), 0)
    fix_top = jnp.where(edge == 0, cw_ref[0:1, :] * (z_prev - z[TM - 1:TM, :]), 0.0)
    fix_bot = jnp.where(edge == SUBLANES - 1, cw_ref[2:3, :] * (z_next - z[0:1, :]), 0.0)
    zc = jnp.concatenate([zc[:SUBLANES] + fix_top, zc[SUBLANES:TM - SUBLANES], zc[TM - SUBLANES:] + fix_bot], axis=0)
    conv_out = _dot((gb_ref[0].astype(F32) * zc).astype(BF16), wco_ref[...])

    merged = (gates_ref[0, :, :D_MODEL].astype(F32) * s5_out
              + gates_ref[0, :, D_MODEL:].astype(F32) * conv_out)
    h = x_ref[0] + _dot(merged.astype(BF16), wo_ref[...])

    hn = _rms(h, gx_ref[...]).astype(BF16)
    q = _dot(hn, wq_ref[...]).astype(BF16)
    heads = []
    for hd in range(XATTN_HEADS):
        hs = slice(hd * XATTN_HEAD_DIM, (hd + 1) * XATTN_HEAD_DIM)
        s = _dot(q[:, hs], kt_ref[0, hs, :]) * (XATTN_HEAD_DIM ** -0.5)
        e = jnp.exp(s - jnp.max(s, axis=-1, keepdims=True))
        p = (e / jnp.sum(e, axis=-1, keepdims=True)).astype(BF16)
        heads.append(_dot(p, v_ref[0, :, hs]).astype(BF16))
    o = jnp.concatenate(heads, axis=-1)
    h_ref[0] = h + _dot(o, wxo_ref[...])


def _mix_attn(x, yp, z, gb, gates, kt, v, w_glu, conv_w, w_conv_out, w_o, norm_xattn_g, w_q, w_xo):
    b, l, _ = x.shape
    n_i = l // TM
    tile = lambda w: pl.BlockSpec((1, TM, w), lambda bi, i: (bi, i, 0))
    planes = pl.BlockSpec((D_S5, TILE_CHUNKS, LANES), lambda bi, i: (0, bi * n_i + i, 0))
    halo = TM // 16
    z_prev = pl.BlockSpec((1, 16, D_CONV), lambda bi, i: (bi, jnp.maximum(i * halo - 1, 0), 0))
    z_next = pl.BlockSpec((1, 16, D_CONV), lambda bi, i: (bi, jnp.minimum((i + 1) * halo, l // 16 - 1), 0))
    per_batch = lambda r, c: pl.BlockSpec((1, r, c), lambda bi, i: (bi, 0, 0))
    return pl.pallas_call(
        _mix_attn_kernel,
        grid=(b, n_i),
        in_specs=[tile(D_MODEL), planes, tile(D_CONV), z_prev, z_next,
                  tile(D_CONV), tile(2 * D_MODEL), per_batch(D_MODEL, N_MEM), per_batch(N_MEM, D_MODEL),
                  _const_spec((D_S5, 2 * D_MODEL)),
                  _const_spec((3, D_CONV)), _const_spec((D_CONV, D_MODEL)), _const_spec((D_MODEL, D_MODEL)),
                  _const_spec((1, D_MODEL)), _const_spec((D_MODEL, D_MODEL)), _const_spec((D_MODEL, D_MODEL))],
        out_specs=tile(D_MODEL),
        out_shape=jax.ShapeDtypeStruct(x.shape, F32),
        compiler_params=pltpu.CompilerParams(dimension_semantics=("parallel", "parallel"),
                                             vmem_limit_bytes=VMEM_LIMIT),
        name="mix_attn",
    )(x, yp, z, z, z, gb, gates, kt, v, w_glu, conv_w, w_conv_out, w_o, norm_xattn_g, w_q, w_xo)


def _ffn_kernel(h_ref, gf_ref, wgu_ref, wd_ref, gout_ref, o_ref):
    h = h_ref[...]
    hn = _rms(h, gf_ref[...]).astype(BF16)
    acc = h
    for lo in range(0, D_FF, FF_CHUNK):
        hi = min(lo + FF_CHUNK, D_FF)
        gate = _dot(hn, wgu_ref[:, lo:hi])
        up = _dot(hn, wgu_ref[:, D_FF + lo:D_FF + hi])
        act = (gate * _sigmoid(gate) * up).astype(BF16)
        acc = acc + _dot(act, wd_ref[lo:hi, :])
    o_ref[...] = _rms(acc, gout_ref[...])


def _ffn_out(h2d, norm_ffn_g, w_gate_up, w_down, norm_final_g):
    n = h2d.shape[0]
    row = pl.BlockSpec((TM, D_MODEL), lambda i: (i, 0))
    return pl.pallas_call(
        _ffn_kernel,
        grid=(n // TM,),
        in_specs=[row, _const_spec((1, D_MODEL)), _const_spec((D_MODEL, 2 * D_FF)),
                  _const_spec((D_FF, D_MODEL)), _const_spec((1, D_MODEL))],
        out_specs=row,
        out_shape=jax.ShapeDtypeStruct(h2d.shape, F32),
        compiler_params=pltpu.CompilerParams(dimension_semantics=("parallel",), vmem_limit_bytes=VMEM_LIMIT),
        name="ffn_out",
    )(h2d, norm_ffn_g, w_gate_up, w_down, norm_final_g)


def _front(x, mem, w):
    b, l, _ = x.shape
    up, z, gb, gates = _in_proj(x.reshape(b * l, D_MODEL), w["norm_mix_g"], w["w_ut"], w["w_in_rest"], w["b_gate"])
    k, v = _mem_kv(mem, w["norm_mem_g"], w["w_kv"])
    shp = lambda a: a.reshape(b, l, a.shape[-1])
    return up, shp(z), shp(gb), shp(gates), jnp.swapaxes(k, 1, 2), v


def _back(x, yp, front, w):
    _, z, gb, gates, kt, v = front
    b, l, _ = x.shape
    h = _mix_attn(x, yp, z, gb, gates, kt, v, w["w_glu"], w["conv_w"], w["w_conv_out"], w["w_o"],
                  w["norm_xattn_g"], w["w_q"], w["w_xo"])
    out = _ffn_out(h.reshape(b * l, D_MODEL), w["norm_ffn_g"], w["w_gate_up"], w["w_down"], w["norm_final_g"])
    return out.reshape(b, l, D_MODEL)


def kernel(x_prompt, x_sample, mem_prompt, mem_sample, norm_mix_g, w_in, b_gate, s5_a_re, s5_a_im, s5_log_dt, s5_b_re, s5_b_im, s5_c_re, s5_c_im, s5_d, w_glu_a, w_glu_b, conv_w, w_conv_out, w_o, norm_xattn_g, norm_mem_g, w_q, w_kv, w_xo, norm_ffn_g, w_gate_up, w_down, norm_final_g):
    row = lambda a: a.reshape(1, -1).astype(F32)
    w = {
        "norm_mix_g": row(norm_mix_g[0]), "w_ut": w_in[0, :, :D_S5].T.astype(BF16),
        "w_in_rest": w_in[0, :, D_S5:].astype(BF16), "b_gate": row(b_gate[0]),
        "w_glu": jnp.concatenate([w_glu_a[0], w_glu_b[0]], axis=1).astype(BF16),
        "conv_w": conv_w[0].astype(F32), "w_conv_out": w_conv_out[0].astype(BF16), "w_o": w_o[0].astype(BF16),
        "norm_xattn_g": row(norm_xattn_g[0]), "norm_mem_g": row(norm_mem_g[0]),
        "w_q": w_q[0].astype(BF16), "w_kv": w_kv[0].astype(BF16), "w_xo": w_xo[0].astype(BF16),
        "norm_ffn_g": row(norm_ffn_g[0]), "w_gate_up": w_gate_up[0].astype(BF16),
        "w_down": w_down[0].astype(BF16), "norm_final_g": row(norm_final_g),
    }
    tables = _s5_tables(s5_a_re[0], s5_a_im[0], s5_log_dt[0], s5_b_re[0], s5_b_im[0], s5_c_re[0], s5_c_im[0])
    skip = jnp.broadcast_to(s5_d[0].astype(F32).reshape(S5_GROUPS, S5_GROUP, 1), (S5_GROUPS, S5_GROUP, LANES))
    fp, fs = _front(x_prompt, mem_prompt, w), _front(x_sample, mem_sample, w)
    ypp, yps = _s5_chunk([fp[0], fs[0]], [x_prompt.shape[1], x_sample.shape[1]], tables + (skip,))
    return _back(x_prompt, ypp, fp, w), _back(x_sample, yps, fs, w)
```

```python
import math

import jax
import jax.numpy as jnp
from jax import lax
from jax.experimental import pallas as pl
from jax.experimental.pallas import tpu as pltpu

F32 = jnp.float32
BF16 = jnp.bfloat16

D_MODEL = 1024
D_S5 = D_MODEL // 2
S5_GROUP = 16
S5_GROUPS = D_S5 // S5_GROUP
S5_STATE = 64
D_CONV = D_MODEL // 2
N_MEM = 256
XATTN_HEADS = 4
XATTN_HEAD_DIM = D_MODEL // XATTN_HEADS
D_FF = -(-8 * D_MODEL // (3 * 256)) * 256
EPS = 1e-6

LANES = 128
SUBLANES = 8
VMEM_LIMIT = 56 * 1024 * 1024

MXU_DIM = 256
TM = 1024
CHUNK = LANES
TILE_CHUNKS = TM // CHUNK
STATE_LANES = 2 * S5_STATE
GK = S5_GROUP * CHUNK
FF_CHUNK = 3 * MXU_DIM


def _const_spec(shape):
    zeros = (0,) * len(shape)
    return pl.BlockSpec(shape, lambda *_: zeros, pipeline_mode=pl.Buffered(1))


def _rms(x, g):
    ms = jnp.mean(x * x, axis=-1, keepdims=True)
    return x * lax.rsqrt(ms + EPS) * g


def _sigmoid(x):
    return 0.5 + 0.5 * jnp.tanh(0.5 * x)


def _gelu_tanh(x):
    c = math.sqrt(2.0 / math.pi)
    return x * (0.5 + 0.5 * jnp.tanh(x * (c + (c * 0.044715) * (x * x))))


def _dot(a, b):
    return jnp.dot(a, b, preferred_element_type=F32)


def _planes_view(ref):
    assert ref.shape[1] == TILE_CHUNKS and ref.shape[2] == LANES
    return ref.reshape(ref.shape[0] * TILE_CHUNKS, LANES)


def _plane_rows(r8, c):
    return pl.ds(r8 * SUBLANES * TILE_CHUNKS + c, SUBLANES, stride=TILE_CHUNKS)


def _in_proj_kernel(x_ref, g_ref, wut_ref, w_ref, bg_ref, up_ref, z_ref, gb_ref, gates_ref):
    xn = _rms(x_ref[...], g_ref[...]).astype(BF16)
    g = _dot(xn, w_ref[:, 3 * D_CONV:])
    gates_ref[...] = _sigmoid(g + bg_ref[...]).astype(BF16)
    ut = lax.dot_general(wut_ref[...], xn, (((1,), (1,)), ((), ())), preferred_element_type=F32)
    planes = _planes_view(up_ref)
    for r8 in range(D_S5 // SUBLANES):
        for c in range(TILE_CHUNKS):
            planes[_plane_rows(r8, c), :] = ut[r8 * SUBLANES:(r8 + 1) * SUBLANES, c * LANES:(c + 1) * LANES]
    x_c = _dot(xn, w_ref[:, 0:D_CONV])
    gate_c = _dot(xn, w_ref[:, 2 * D_CONV:3 * D_CONV])
    z_ref[...] = (gate_c * x_c).astype(BF16)
    gb_ref[...] = _dot(xn, w_ref[:, D_CONV:2 * D_CONV]).astype(BF16)


def _in_proj(x2d, norm_g, w_ut, w_rest, b_gate):
    n = x2d.shape[0]
    row = lambda w: pl.BlockSpec((TM, w), lambda i: (i, 0))
    return pl.pallas_call(
        _in_proj_kernel,
        grid=(n // TM,),
        in_specs=[row(D_MODEL), _const_spec((1, D_MODEL)), _const_spec(w_ut.shape), _const_spec(w_rest.shape),
                  _const_spec((1, 2 * D_MODEL))],
        out_specs=[pl.BlockSpec((D_S5, TILE_CHUNKS, LANES), lambda i: (0, i, 0)), row(D_CONV), row(D_CONV),
                   row(2 * D_MODEL)],
        out_shape=[jax.ShapeDtypeStruct((D_S5, n // CHUNK, LANES), F32), jax.ShapeDtypeStruct((n, D_CONV), BF16),
                   jax.ShapeDtypeStruct((n, D_CONV), BF16), jax.ShapeDtypeStruct((n, 2 * D_MODEL), BF16)],
        compiler_params=pltpu.CompilerParams(dimension_semantics=("parallel",), vmem_limit_bytes=VMEM_LIMIT),
        name="in_proj",
    )(x2d, norm_g, w_ut, w_rest, b_gate)


def _mem_kv_kernel(m_ref, g_ref, w_ref, k_ref, v_ref):
    mn = _rms(m_ref[0], g_ref[...]).astype(BF16)
    k_ref[0] = _dot(mn, w_ref[:, :D_MODEL]).astype(BF16)
    v_ref[0] = _dot(mn, w_ref[:, D_MODEL:]).astype(BF16)


def _mem_kv(mem, norm_g, w_kv):
    b = mem.shape[0]
    blk = pl.BlockSpec((1, N_MEM, D_MODEL), lambda i: (i, 0, 0))
    return pl.pallas_call(
        _mem_kv_kernel,
        grid=(b,),
        in_specs=[blk, _const_spec((1, D_MODEL)), _const_spec((D_MODEL, 2 * D_MODEL))],
        out_specs=[blk, blk],
        out_shape=[jax.ShapeDtypeStruct((b, N_MEM, D_MODEL), BF16)] * 2,
        compiler_params=pltpu.CompilerParams(dimension_semantics=("parallel",), vmem_limit_bytes=VMEM_LIMIT),
        name="mem_kv",
    )(mem, norm_g, w_kv)


def _s5_tables(a_re, a_im, log_dt, b_re, b_im, c_re, c_im):
    f = lambda v: v.astype(F32)
    a_re, a_im, b_re, b_im, c_re, c_im = map(f, (a_re, a_im, b_re, b_im, c_re, c_im))
    dt = jnp.exp(f(log_dt))[..., None]
    lr, li = a_re * dt, a_im * dt

    def apow(k):
        m = jnp.exp(lr[..., None] * k)
        return m * jnp.cos(li[..., None] * k), m * jnp.sin(li[..., None] * k)

    ab_re, ab_im = jnp.exp(lr) * jnp.cos(li), jnp.exp(lr) * jnp.sin(li)
    x, y, den = ab_re - 1.0, ab_im, a_re * a_re + a_im * a_im
    cf_re, cf_im = (x * a_re + y * a_im) / den, (y * a_re - x * a_im) / den
    bb_re = cf_re[..., None] * b_re - cf_im[..., None] * b_im
    bb_im = cf_re[..., None] * b_im + cf_im[..., None] * b_re
    bq_re, bq_im = jnp.swapaxes(bb_re, -1, -2), jnp.swapaxes(bb_im, -1, -2)
    g = a_re.shape[1]

    cbr = c_re[:, :, None] * bq_re[:, :, :, None] - c_im[:, :, None] * bq_im[:, :, :, None]
    cbi = c_re[:, :, None] * bq_im[:, :, :, None] + c_im[:, :, None] * bq_re[:, :, :, None]
    half = jnp.concatenate([cbr, -cbi], axis=-1).reshape(2, g, S5_GROUP * S5_GROUP, STATE_LANES)
    cb = jnp.concatenate([half[0], half[1]], axis=-1)
    j = jnp.arange(2 * CHUNK, dtype=F32)
    pf_re, pf_im = apow(jnp.maximum(j - CHUNK, 0.0))
    pb_re, pb_im = apow(jnp.maximum(CHUNK - j, 0.0))
    mf, mb = (j >= CHUNK).astype(F32), (j <= CHUNK).astype(F32)
    pw = jnp.concatenate([pf_re[0] * mf, pf_im[0] * mf, pb_re[1] * mb, pb_im[1] * mb], axis=1)

    s = jnp.arange(CHUNK, dtype=F32)
    in_re, in_im = apow(jnp.stack([CHUNK - 1.0 - s, s])[:, None, None, :])
    out_re, out_im = apow(jnp.stack([s + 1.0, CHUNK - s])[:, None, None, :])
    both = lambda re, im: jnp.concatenate([re[0], im[0], re[1], im[1]], axis=-1)
    tr = lambda v: jnp.swapaxes(v, -1, -2)
    gen = jnp.concatenate([both(bq_re, bq_re), both(-bq_im, bq_im), both(c_re, -c_re), both(-c_im, -c_im),
                           both(tr(in_re), tr(in_im)), both(tr(out_re), tr(out_im))], axis=1)

    ks = CHUNK * (2.0 ** jnp.arange(6, dtype=F32))
    t_re, t_im = apow(ks)
    t_re, t_im = jnp.moveaxis(t_re, -1, 2), jnp.moveaxis(t_im, -1, 2)
    row_a = jnp.concatenate([t_re[0], t_re[0], t_re[1], t_re[1]], axis=-1)
    row_b = jnp.concatenate([-t_im[0], t_im[0], -t_im[1], t_im[1]], axis=-1)
    step = jnp.concatenate([row_a, row_b], axis=1)
    return cb, pw, gen, step


def _make_s5_kernel(seqs):
    nc = sum(n for n, _ in seqs)
    starts = [sum(n for n, _ in seqs[:k]) for k in range(len(seqs))]

    def kernel(*refs):
        n_in = len(seqs)
        up_refs = refs[:n_in]
        cb_ref, pw_ref, cbn_ref, pwn_ref, gen_ref, step_ref, d_ref = refs[n_in:n_in + 7]
        yp_refs = refs[n_in + 7:2 * n_in + 7]
        xcat, toep, lag, ws, wct = refs[2 * n_in + 7:]
        g = pl.program_id(0)
        slot = g % 2

        for ref, start, (n, _) in zip(up_refs, starts, seqs):
            for q in range(S5_GROUP):
                xcat[start:start + n, q * LANES:(q + 1) * LANES] = ref[q].astype(BF16)

        s_idx = lax.broadcasted_iota(jnp.int32, (CHUNK, CHUNK), 0)
        j_idx = lax.broadcasted_iota(jnp.int32, (CHUNK, CHUNK), 1)
        from_fwd = j_idx + s_idx < CHUNK

        def toeplitz(cb, pw, dst):
            lag[...] = jnp.dot(cb[0], pw[0], preferred_element_type=F32, precision=lax.Precision.HIGHEST)
            for q in range(S5_GROUP):
                for p in range(S5_GROUP):
                    r = q * S5_GROUP + p
                    mixed = jnp.where(from_fwd, lag[r:r + 1, CHUNK:], lag[r:r + 1, :CHUNK])
                    skew = pltpu.roll(mixed, 0, axis=1, stride=1, stride_axis=0)
                    dst[q * CHUNK:(q + 1) * CHUNK, p * LANES:(p + 1) * LANES] = skew.astype(BF16)

        @pl.when(g == 0)
        def _():
            toeplitz(cb_ref, pw_ref, toep.at[0])

        toeplitz(cbn_ref, pwn_ref, toep.at[1 - slot])

        def outer(dst, f1, f2, powers):
            swapped = jnp.concatenate([pltpu.roll(powers[:, :STATE_LANES], S5_STATE, axis=1),
                                       pltpu.roll(powers[:, STATE_LANES:], S5_STATE, axis=1)], axis=1)
            for ch in range(S5_GROUP):
                blk = gen_ref[0, f1 + ch:f1 + ch + 1, :] * powers + gen_ref[0, f2 + ch:f2 + ch + 1, :] * swapped
                dst[ch * CHUNK:(ch + 1) * CHUNK, :] = blk.astype(BF16)

        outer(ws, 0, S5_GROUP, gen_ref[0, 4 * S5_GROUP:4 * S5_GROUP + CHUNK, :])
        outer(wct, 2 * S5_GROUP, 3 * S5_GROUP, gen_ref[0, 4 * S5_GROUP + CHUNK:, :])

        x = xcat[...]
        y = _dot(x, toep[slot])
        summ = _dot(x, ws[...])

        idx = lax.broadcasted_iota(jnp.int32, (nc, LANES), 0)
        pos, left, start = idx, idx, 0
        for n, per_seq in seqs:
            inside = (idx >= start) & (idx < start + n)
            p_ = (idx - start) & (per_seq - 1)
            pos = jnp.where(inside, p_, pos)
            left = jnp.where(inside, per_seq - 1 - p_, left)
            start += n
        max_seq = max(per_seq for _, per_seq in seqs)

        def scan(h, col, behind, shift_sign):
            k, i = 1, 0
            while k < max_seq:
                nb = pltpu.roll(h, (shift_sign * k) % nc, axis=0)
                swapped = pltpu.roll(nb, S5_STATE, axis=1)
                a = step_ref[0, i:i + 1, col:col + STATE_LANES]
                b = step_ref[0, 6 + i:7 + i, col:col + STATE_LANES]
                h = h + jnp.where(behind >= k, a * nb + b * swapped, 0.0)
                k, i = 2 * k, i + 1
            carried = pltpu.roll(h, shift_sign % nc, axis=0)
            return jnp.where(behind >= 1, carried, 0.0)

        h_prev = scan(summ[:, :STATE_LANES], 0, pos, 1)
        h_next = scan(summ[:, STATE_LANES:], STATE_LANES, left, -1)
        carried = jnp.concatenate([h_prev, h_next], axis=1).astype(BF16)
        y = y + lax.dot_general(carried, wct[...], (((1,), (1,)), ((), ())), preferred_element_type=F32)

        for u_ref, y_ref, start, (n, _) in zip(up_refs, yp_refs, starts, seqs):
            for p in range(S5_GROUP):
                y_ref[p] = y[start:start + n, p * LANES:(p + 1) * LANES] + d_ref[0, p:p + 1, :] * u_ref[p]

    return kernel, nc


def _s5_chunk(ups, seq_lens, tables):
    seqs = []
    for up, l in zip(ups, seq_lens):
        per_seq = l // CHUNK
        assert l % CHUNK == 0 and per_seq & (per_seq - 1) == 0 and up.shape[1] % per_seq == 0
        seqs.append((up.shape[1], per_seq))
    kernel, nc = _make_s5_kernel(tuple(seqs))
    per_group = lambda a: pl.BlockSpec((1,) + a.shape[1:], lambda g: (g, 0, 0))
    next_group = lambda a: pl.BlockSpec((1,) + a.shape[1:], lambda g: (jnp.minimum(g + 1, S5_GROUPS - 1), 0, 0))
    io = [pl.BlockSpec((S5_GROUP,) + up.shape[1:], lambda g: (g, 0, 0)) for up in ups]
    cb, pw = tables[:2]
    return pl.pallas_call(
        kernel,
        grid=(S5_GROUPS,),
        in_specs=io + [per_group(cb), per_group(pw), next_group(cb), next_group(pw)] + [per_group(t) for t in tables[2:]],
        out_specs=io,
        out_shape=[jax.ShapeDtypeStruct(up.shape, F32) for up in ups],
        scratch_shapes=[pltpu.VMEM((nc, GK), BF16), pltpu.VMEM((2, GK, GK), BF16),
                        pltpu.VMEM((S5_GROUP * S5_GROUP, 2 * CHUNK), F32),
                        pltpu.VMEM((GK, 2 * STATE_LANES), BF16), pltpu.VMEM((GK, 2 * STATE_LANES), BF16)],
        compiler_params=pltpu.CompilerParams(dimension_semantics=("arbitrary",), vmem_limit_bytes=VMEM_LIMIT),
        name="s5_chunk",
    )(*ups, cb, pw, cb, pw, *tables[2:])


def _mix_attn_kernel(x_ref, yp_ref, z_ref, zprev_ref, znext_ref, gb_ref, gates_ref,
                     kt_ref, v_ref, wglu_ref, cw_ref, wco_ref, wo_ref, gx_ref, wq_ref,
                     wxo_ref, h_ref):
    i = pl.program_id(1)
    n_i = pl.num_programs(1)

    planes = _planes_view(yp_ref)
    yt = jnp.concatenate(
        [jnp.concatenate([planes[_plane_rows(r8, c), :] for c in range(TILE_CHUNKS)], axis=1)
         for r8 in range(D_S5 // SUBLANES)], axis=0)
    yt = _gelu_tanh(yt).astype(BF16)
    glu = lax.dot_general(yt, wglu_ref[...], (((0,), (0,)), ((), ())), preferred_element_type=F32)
    s5_out = glu[:, :D_MODEL] * _sigmoid(glu[:, D_MODEL:])

    z = z_ref[0].astype(F32)
    z_prev = jnp.where(i > 0, zprev_ref[0, 15:16, :].astype(F32), 0.0)
    z_next = jnp.where(i < n_i - 1, znext_ref[0, 0:1, :].astype(F32), 0.0)
    zc = (cw_ref[0:1, :] * pltpu.roll(z, 1, axis=0) + cw_ref[1:2, :] * z
          + cw_ref[2:3, :] * pltpu.roll(z, TM - 1, axis=0))
    edge = lax.broadcasted_iota(jnp.int32, (SUBLANES, D_CONV), 0)
    fix_top = jnp.where(edge == 0, cw_ref[0:1, :] * (z_prev - z[TM - 1:TM, :]), 0.0)
    fix_bot = jnp.where(edge == SUBLANES - 1, cw_ref[2:3, :] * (z_next - z[0:1, :]), 0.0)
    zc = jnp.concatenate([zc[:SUBLANES] + fix_top, zc[SUBLANES:TM - SUBLANES], zc[TM - SUBLANES:] + fix_bot], axis=0)
    conv_out = _dot((gb_ref[0].astype(F32) * zc).astype(BF16), wco_ref[...])

    merged = (gates_ref[0, :, :D_MODEL].astype(F32) * s5_out
              + gates_ref[0, :, D_MODEL:].astype(F32) * conv_out)
    h = x_ref[0] + _dot(merged.astype(BF16), wo_ref[...])

    hn = _rms(h, gx_ref[...]).astype(BF16)
    q = _dot(hn, wq_ref[...]).astype(BF16)
    heads = []
    for hd in range(XATTN_HEADS):
        hs = slice(hd * XATTN_HEAD_DIM, (hd + 1) * XATTN_HEAD_DIM)
        s = _dot(q[:, hs], kt_ref[0, hs, :]) * (XATTN_HEAD_DIM ** -0.5)
        e = jnp.exp(s - jnp.max(s, axis=-1, keepdims=True))
        p = (e * (1.0 / jnp.sum(e, axis=-1, keepdims=True))).astype(BF16)
        heads.append(_dot(p, v_ref[0, :, hs]).astype(BF16))
    o = jnp.concatenate(heads, axis=-1)
    h_ref[0] = h + _dot(o, wxo_ref[...])


def _mix_attn(x, yp, z, gb, gates, kt, v, w_glu, conv_w, w_conv_out, w_o, norm_xattn_g, w_q, w_xo):
    b, l, _ = x.shape
    n_i = l // TM
    tile = lambda w: pl.BlockSpec((1, TM, w), lambda bi, i: (bi, i, 0))
    planes = pl.BlockSpec((D_S5, TILE_CHUNKS, LANES), lambda bi, i: (0, bi * n_i + i, 0))
    halo = TM // 16
    z_prev = pl.BlockSpec((1, 16, D_CONV), lambda bi, i: (bi, jnp.maximum(i * halo - 1, 0), 0))
    z_next = pl.BlockSpec((1, 16, D_CONV), lambda bi, i: (bi, jnp.minimum((i + 1) * halo, l // 16 - 1), 0))
    per_batch = lambda r, c: pl.BlockSpec((1, r, c), lambda bi, i: (bi, 0, 0))
    return pl.pallas_call(
        _mix_attn_kernel,
        grid=(b, n_i),
        in_specs=[tile(D_MODEL), planes, tile(D_CONV), z_prev, z_next,
                  tile(D_CONV), tile(2 * D_MODEL), per_batch(D_MODEL, N_MEM), per_batch(N_MEM, D_MODEL),
                  _const_spec((D_S5, 2 * D_MODEL)),
                  _const_spec((3, D_CONV)), _const_spec((D_CONV, D_MODEL)), _const_spec((D_MODEL, D_MODEL)),
                  _const_spec((1, D_MODEL)), _const_spec((D_MODEL, D_MODEL)), _const_spec((D_MODEL, D_MODEL))],
        out_specs=tile(D_MODEL),
        out_shape=jax.ShapeDtypeStruct(x.shape, F32),
        compiler_params=pltpu.CompilerParams(dimension_semantics=("parallel", "parallel"),
                                             vmem_limit_bytes=VMEM_LIMIT),
        name="mix_attn",
    )(x, yp, z, z, z, gb, gates, kt, v, w_glu, conv_w, w_conv_out, w_o, norm_xattn_g, w_q, w_xo)


def _ffn_kernel(h_ref, gf_ref, wgu_ref, wd_ref, gout_ref, o_ref):
    h = h_ref[...]
    hn = _rms(h, gf_ref[...]).astype(BF16)
    acc = h
    for lo in range(0, D_FF, FF_CHUNK):
        hi = min(lo + FF_CHUNK, D_FF)
        gate = _dot(hn, wgu_ref[:, lo:hi])
        up = _dot(hn, wgu_ref[:, D_FF + lo:D_FF + hi])
        act = (gate * _sigmoid(gate) * up).astype(BF16)
        acc = acc + _dot(act, wd_ref[lo:hi, :])
    o_ref[...] = _rms(acc, gout_ref[...])


def _ffn_out(h2d, norm_ffn_g, w_gate_up, w_down, norm_final_g):
    n = h2d.shape[0]
    row = pl.BlockSpec((TM, D_MODEL), lambda i: (i, 0))
    return pl.pallas_call(
        _ffn_kernel,
        grid=(n // TM,),
        in_specs=[row, _const_spec((1, D_MODEL)), _const_spec((D_MODEL, 2 * D_FF)),
                  _const_spec((D_FF, D_MODEL)), _const_spec((1, D_MODEL))],
        out_specs=row,
        out_shape=jax.ShapeDtypeStruct(h2d.shape, F32),
        compiler_params=pltpu.CompilerParams(dimension_semantics=("parallel",), vmem_limit_bytes=VMEM_LIMIT),
        name="ffn_out",
    )(h2d, norm_ffn_g, w_gate_up, w_down, norm_final_g)


def _front(x, mem, w):
    b, l, _ = x.shape
    up, z, gb, gates = _in_proj(x.reshape(b * l, D_MODEL), w["norm_mix_g"], w["w_ut"], w["w_in_rest"], w["b_gate"])
    k, v = _mem_kv(mem, w["norm_mem_g"], w["w_kv"])
    shp = lambda a: a.reshape(b, l, a.shape[-1])
    return up, shp(z), shp(gb), shp(gates), jnp.swapaxes(k, 1, 2), v


def _back(x, yp, front, w):
    _, z, gb, gates, kt, v = front
    b, l, _ = x.shape
    h = _mix_attn(x, yp, z, gb, gates, kt, v, w["w_glu"], w["conv_w"], w["w_conv_out"], w["w_o"],
                  w["norm_xattn_g"], w["w_q"], w["w_xo"])
    out = _ffn_out(h.reshape(b * l, D_MODEL), w["norm_ffn_g"], w["w_gate_up"], w["w_down"], w["norm_final_g"])
    return out.reshape(b, l, D_MODEL)


def kernel(x_prompt, x_sample, mem_prompt, mem_sample, norm_mix_g, w_in, b_gate, s5_a_re, s5_a_im, s5_log_dt, s5_b_re, s5_b_im, s5_c_re, s5_c_im, s5_d, w_glu_a, w_glu_b, conv_w, w_conv_out, w_o, norm_xattn_g, norm_mem_g, w_q, w_kv, w_xo, norm_ffn_g, w_gate_up, w_down, norm_final_g):
    row = lambda a: a.reshape(1, -1).astype(F32)
    w = {
        "norm_mix_g": row(norm_mix_g[0]), "w_ut": w_in[0, :, :D_S5].T.astype(BF16),
        "w_in_rest": w_in[0, :, D_S5:].astype(BF16), "b_gate": row(b_gate[0]),
        "w_glu": jnp.concatenate([w_glu_a[0], w_glu_b[0]], axis=1).astype(BF16),
        "conv_w": conv_w[0].astype(F32), "w_conv_out": w_conv_out[0].astype(BF16), "w_o": w_o[0].astype(BF16),
        "norm_xattn_g": row(norm_xattn_g[0]), "norm_mem_g": row(norm_mem_g[0]),
        "w_q": w_q[0].astype(BF16), "w_kv": w_kv[0].astype(BF16), "w_xo": w_xo[0].astype(BF16),
        "norm_ffn_g": row(norm_ffn_g[0]), "w_gate_up": w_gate_up[0].astype(BF16),
        "w_down": w_down[0].astype(BF16), "norm_final_g": row(norm_final_g),
    }
    tables = _s5_tables(s5_a_re[0], s5_a_im[0], s5_log_dt[0], s5_b_re[0], s5_b_im[0], s5_c_re[0], s5_c_im[0])
    skip = jnp.broadcast_to(s5_d[0].astype(F32).reshape(S5_GROUPS, S5_GROUP, 1), (S5_GROUPS, S5_GROUP, LANES))
    fp, fs = _front(x_prompt, mem_prompt, w), _front(x_sample, mem_sample, w)
    ypp, yps = _s5_chunk([fp[0], fs[0]], [x_prompt.shape[1], x_sample.shape[1]], tables + (skip,))
    return _back(x_prompt, ypp, fp, w), _back(x_sample, yps, fs, w)
```
